```python
import math
import jax, jax.numpy as jnp
from jax import lax
import numpy as np

D_MODEL = 1024
BATCH = 16
SEQ = 2048
DEPTH = 4

N_MIXERS = 4
BLOCK = 128
EPS = 1e-6
SB_HEADS = 16
SB_HEAD_DIM = D_MODEL // SB_HEADS
SB_WIDTH = SB_HEADS * SB_HEAD_DIM
GM_WIDTH = 2 * D_MODEL
GM_CHUNK = 128
GM_GROUPS = 16
CV_WIDTH = 2 * D_MODEL
CV_KERNEL = 31
FX_HEADS = 16
FX_HEAD_DIM = D_MODEL // FX_HEADS
FX_WIDTH = FX_HEADS * FX_HEAD_DIM

kernel_name = "hybrid_sb_gmlp_conv_fox_trunk"


def _layer_counts():
    return [len(range(kind, DEPTH, N_MIXERS)) for kind in range(N_MIXERS)]


def rmsnorm(x, g):
    xf = x.astype(jnp.float32)
    y = xf * lax.rsqrt(jnp.mean(xf * xf, axis=-1, keepdims=True) + EPS)
    return (y * g.astype(jnp.float32)).astype(x.dtype)


def layernorm(x, g, b):
    xf = x.astype(jnp.float32)
    mu = jnp.mean(xf, axis=-1, keepdims=True)
    xc = xf - mu
    var = jnp.mean(xc * xc, axis=-1, keepdims=True)
    y = xc * lax.rsqrt(var + EPS) * g.astype(jnp.float32) + b.astype(jnp.float32)
    return y.astype(x.dtype)


def stick_breaking_attention(q, k, v):
    B, S, H, Dh = q.shape
    scale = 1.0 / math.sqrt(Dh)
    outs = []
    for t0 in range(0, S, BLOCK):
        kend = t0 + BLOCK
        z = jnp.einsum('bthd,bshd->bhts', q[:, t0:kend], k[:, :kend]).astype(jnp.float32) * scale
        t_idx = t0 + jnp.arange(BLOCK)[:, None]
        s_idx = jnp.arange(kend)[None, :]
        mask = s_idx < t_idx
        log_beta = jax.nn.log_sigmoid(z)
        log_rem = jnp.where(mask, jax.nn.log_sigmoid(-z), 0.0)
        after = lax.cumsum(log_rem, axis=3, reverse=True) - log_rem
        w = jnp.where(mask, jnp.exp(log_beta + after), 0.0)
        outs.append(jnp.einsum('bhts,bshd->bthd', w.astype(v.dtype), v[:, :kend]))
    return jnp.concatenate(outs, axis=1)


def forgetting_attention(q, k, v, log_f):
    B, S, H, Dh = q.shape
    scale = 1.0 / math.sqrt(Dh)
    cum = jnp.cumsum(log_f, axis=1).transpose(0, 2, 1)
    outs = []
    for t0 in range(0, S, BLOCK):
        kend = t0 + BLOCK
        logits = jnp.einsum('bthd,bshd->bhts', q[:, t0:kend], k[:, :kend]).astype(jnp.float32) * scale
        logits = logits + cum[:, :, t0:kend, None] - cum[:, :, None, :kend]
        t_idx = t0 + jnp.arange(BLOCK)[:, None]
        s_idx = jnp.arange(kend)[None, :]
        p = jax.nn.softmax(jnp.where(s_idx <= t_idx, logits, -jnp.inf), axis=-1)
        outs.append(jnp.einsum('bhts,bshd->bthd', p.astype(v.dtype), v[:, :kend]))
    return jnp.concatenate(outs, axis=1)


def mixer_stick_breaking(h, w_in, w_out):
    B, S, _ = h.shape
    proj = jnp.einsum('bsd,de->bse', h, w_in)
    q, k, v, g = jnp.split(proj, 4, axis=-1)
    hs = (B, S, SB_HEADS, SB_HEAD_DIM)
    o = stick_breaking_attention(q.reshape(hs), k.reshape(hs), v.reshape(hs)).reshape(B, S, SB_WIDTH)
    return jnp.einsum('bse,ed->bsd', o * jax.nn.silu(g), w_out)


def mixer_chunked_gmlp(h, w_in, v_ln_g, v_ln_b, w_s, b_s, w_out):
    B, S, _ = h.shape
    proj = jnp.einsum('bsd,de->bse', h, w_in)
    uv, g = proj[..., :2 * GM_WIDTH], proj[..., 2 * GM_WIDTH:]
    uv = jax.nn.gelu(uv)
    u, v = jnp.split(uv, 2, axis=-1)
    v = layernorm(v, v_ln_g, v_ln_b)
    tril = jnp.tril(jnp.ones((GM_CHUNK, GM_CHUNK), w_s.dtype))
    vc = v.reshape(B, S // GM_CHUNK, GM_CHUNK, GM_GROUPS, GM_WIDTH // GM_GROUPS)
    s = jnp.einsum('gts,bnsgc->bntgc', w_s * tril, vc) + b_s.T[None, None, :, :, None]
    o = u * s.reshape(B, S, GM_WIDTH)
    return jnp.einsum('bse,ed->bsd', o * jax.nn.silu(g), w_out)


def mixer_conformer_conv(h, w_in, conv_w, conv_b, ln_g, ln_b, w_out):
    proj = jnp.einsum('bsd,de->bse', h, w_in)
    a, b, g = jnp.split(proj, 3, axis=-1)
    y = a * jax.nn.sigmoid(b)
    y = lax.conv_general_dilated(
        y, conv_w[:, None, :], window_strides=(1,), padding=[(CV_KERNEL - 1, 0)],
        dimension_numbers=('NWC', 'WIO', 'NWC'), feature_group_count=CV_WIDTH) + conv_b
    y = jax.nn.silu(layernorm(y, ln_g, ln_b))
    return jnp.einsum('bse,ed->bsd', y * jax.nn.silu(g), w_out)


def mixer_forgetting(h, w_in, b_f, w_out):
    B, S, _ = h.shape
    proj = jnp.einsum('bsd,de->bse', h, w_in)
    q = proj[..., :FX_WIDTH]
    k = proj[..., FX_WIDTH:2 * FX_WIDTH]
    v = proj[..., 2 * FX_WIDTH:3 * FX_WIDTH]
    g = proj[..., 3 * FX_WIDTH:4 * FX_WIDTH]
    f_logit = proj[..., 4 * FX_WIDTH:].astype(jnp.float32) + b_f.astype(jnp.float32)
    log_f = jax.nn.log_sigmoid(f_logit)
    hs = (B, S, FX_HEADS, FX_HEAD_DIM)
    o = forgetting_attention(q.reshape(hs), k.reshape(hs), v.reshape(hs), log_f).reshape(B, S, FX_WIDTH)
    return jnp.einsum('bse,ed->bsd', o * jax.nn.silu(g), w_out)


def setup_inputs(seed: int = 0) -> dict:
    key = jax.random.key(seed)
    ks = iter(jax.random.split(key, 32))
    nA, nB, nC, nD = _layer_counts()
    f32 = jnp.float32

    def nrm(shape, scale):
        return jax.random.normal(next(ks), shape, f32) * scale

    def gain(n, width):
        return 1.0 + nrm((n, width), 0.02)

    d_in = D_MODEL ** -0.5
    return {
        "x": nrm((BATCH, SEQ, D_MODEL), 1.0),
        "a_norm": gain(nA, D_MODEL),
        "a_w_in": nrm((nA, D_MODEL, 4 * SB_WIDTH), d_in),
        "a_w_out": nrm((nA, SB_WIDTH, D_MODEL), SB_WIDTH ** -0.5),
        "b_norm": gain(nB, D_MODEL),
        "b_w_in": nrm((nB, D_MODEL, 3 * GM_WIDTH), d_in),
        "b_v_ln_g": gain(nB, GM_WIDTH),
        "b_v_ln_b": nrm((nB, GM_WIDTH), 0.02),
        "b_w_s": nrm((nB, GM_GROUPS, GM_CHUNK, GM_CHUNK), GM_CHUNK ** -0.5),
        "b_b_s": 1.0 + nrm((nB, GM_GROUPS, GM_CHUNK), 0.1),
        "b_w_out": nrm((nB, GM_WIDTH, D_MODEL), GM_WIDTH ** -0.5),
        "c_norm": gain(nC, D_MODEL),
        "c_w_in": nrm((nC, D_MODEL, 3 * CV_WIDTH), d_in),
        "c_conv_w": nrm((nC, CV_KERNEL, CV_WIDTH), CV_KERNEL ** -0.5),
        "c_conv_b": nrm((nC, CV_WIDTH), 0.02),
        "c_ln_g": gain(nC, CV_WIDTH),
        "c_ln_b": nrm((nC, CV_WIDTH), 0.02),
        "c_w_out": nrm((nC, CV_WIDTH, D_MODEL), CV_WIDTH ** -0.5),
        "d_norm": gain(nD, D_MODEL),
        "d_w_in": nrm((nD, D_MODEL, 4 * FX_WIDTH + FX_HEADS), d_in),
        "d_b_f": 3.0 + nrm((nD, FX_HEADS), 0.5),
        "d_w_out": nrm((nD, FX_WIDTH, D_MODEL), FX_WIDTH ** -0.5),
        "final_norm": 1.0 + nrm((D_MODEL,), 0.02),
    }


def reference(x, a_norm, a_w_in, a_w_out, b_norm, b_w_in, b_v_ln_g, b_v_ln_b, b_w_s, b_b_s, b_w_out,
              c_norm, c_w_in, c_conv_w, c_conv_b, c_ln_g, c_ln_b, c_w_out,
              d_norm, d_w_in, d_b_f, d_w_out, final_norm):
    for i in range(DEPTH):
        kind, j = i % N_MIXERS, i // N_MIXERS
        if kind == 0:
            x = x + mixer_stick_breaking(rmsnorm(x, a_norm[j]), a_w_in[j], a_w_out[j])
        elif kind == 1:
            x = x + mixer_chunked_gmlp(rmsnorm(x, b_norm[j]), b_w_in[j], b_v_ln_g[j], b_v_ln_b[j],
                                       b_w_s[j], b_b_s[j], b_w_out[j])
        elif kind == 2:
            x = x + mixer_conformer_conv(rmsnorm(x, c_norm[j]), c_w_in[j], c_conv_w[j], c_conv_b[j],
                                         c_ln_g[j], c_ln_b[j], c_w_out[j])
        else:
            x = x + mixer_forgetting(rmsnorm(x, d_norm[j]), d_w_in[j], d_b_f[j], d_w_out[j])
    return rmsnorm(x, final_norm)
```

```python
import functools
import math

import jax
import jax.numpy as jnp
from jax import lax
from jax.experimental import pallas as pl
from jax.experimental.pallas import tpu as pltpu

EPS = 1e-6
LANES = 128
HEAD_DIM = 64
GM_CHUNK = 128
GM_GROUPS = 16
CV_KERNEL = 31
HALO = 32
TQ = 128
TK = 256
VMEM_LIMIT = 56 * 1024 * 1024

F32 = jnp.float32
BF16 = jnp.bfloat16


def _resident(shape):
    nd = len(shape)
    return pl.BlockSpec(shape, lambda *_: (0,) * nd, pipeline_mode=pl.Buffered(1))


def _rms(x, gain):
    return x * lax.rsqrt(jnp.mean(x * x, axis=-1, keepdims=True) + EPS) * gain


def _sigmoid(x):
    return 1.0 / (1.0 + jnp.exp(-x))


def _silu(x):
    return x * _sigmoid(x)


def _gelu_tanh(x):
    c = math.sqrt(2.0 / math.pi)
    return 0.5 * x * (1.0 + jnp.tanh(c * (x + 0.044715 * (x * x * x))))


def _log_sigmoid(x):
    return jnp.minimum(x, 0.0) - jnp.log(1.0 + jnp.exp(-jnp.abs(x)))


def _dot(a, b):
    return jnp.dot(a, b, preferred_element_type=F32)


def _dot_nt(a, b):
    return lax.dot_general(a, b, (((1,), (1,)), ((), ())), preferred_element_type=F32)


def _norm_proj_kernel(x_ref, gain_ref, w_ref, o_ref, *, n_chunk):
    hn = _rms(x_ref[...], gain_ref[...]).astype(BF16)
    for c in range(0, o_ref.shape[1], n_chunk):
        o_ref[:, c:c + n_chunk] = _dot(hn, w_ref[:, c:c + n_chunk]).astype(o_ref.dtype)


def _norm_proj_gate_kernel(x_ref, gain_ref, w_ref, wf_ref, bf_ref, o_ref, lf_ref, *, n_chunk):
    hn = _rms(x_ref[...], gain_ref[...]).astype(BF16)
    for c in range(0, o_ref.shape[1], n_chunk):
        o_ref[:, c:c + n_chunk] = _dot(hn, w_ref[:, c:c + n_chunk]).astype(o_ref.dtype)
    lf_ref[...] = _log_sigmoid(_dot(hn, wf_ref[...]) + bf_ref[...])


def _norm_proj(x2, gain, w, tm=512, n_chunk=1024):
    n, d = x2.shape
    e = w.shape[1]
    return pl.pallas_call(
        functools.partial(_norm_proj_kernel, n_chunk=n_chunk),
        grid=(n // tm,),
        in_specs=[pl.BlockSpec((tm, d), lambda i: (i, 0)), _resident((1, d)), _resident((d, e))],
        out_specs=pl.BlockSpec((tm, e), lambda i: (i, 0)),
        out_shape=jax.ShapeDtypeStruct((n, e), BF16),
        compiler_params=pltpu.CompilerParams(dimension_semantics=("arbitrary",),
                                             vmem_limit_bytes=VMEM_LIMIT),
        name="norm_proj",
    )(x2, gain, w)


def _norm_proj_gate(x2, gain, w, wf, bf, tm=512, n_chunk=1024):
    n, d = x2.shape
    e = w.shape[1]
    return pl.pallas_call(
        functools.partial(_norm_proj_gate_kernel, n_chunk=n_chunk),
        grid=(n // tm,),
        in_specs=[pl.BlockSpec((tm, d), lambda i: (i, 0)), _resident((1, d)), _resident((d, e)),
                  _resident((d, LANES)), _resident((1, LANES))],
        out_specs=[pl.BlockSpec((tm, e), lambda i: (i, 0)),
                   pl.BlockSpec((tm, LANES), lambda i: (i, 0))],
        out_shape=[jax.ShapeDtypeStruct((n, e), BF16), jax.ShapeDtypeStruct((n, LANES), F32)],
        compiler_params=pltpu.CompilerParams(dimension_semantics=("arbitrary",),
                                             vmem_limit_bytes=VMEM_LIMIT),
        name="norm_proj_gate",
    )(x2, gain, w, wf, bf)


def _split3(x):
    x1 = x.astype(BF16)
    r1 = x - x1.astype(F32)
    x2 = r1.astype(BF16)
    x3 = (r1 - x2.astype(F32)).astype(BF16)
    return x1, x2, x3


def _cumsum_kernel(lf_ref, o_ref):
    s, h = lf_ref.shape
    row = lax.broadcasted_iota(jnp.int32, (TK, TK), 0)
    col = lax.broadcasted_iota(jnp.int32, (TK, TK), 1)
    lower = jnp.where(col <= row, 1.0, 0.0).astype(BF16)
    carry = jnp.zeros((1, h), F32)
    for c in range(0, s, TK):
        x = lf_ref[c:c + TK, :]
        x1, x2, x3 = _split3(x)
        cs = _dot(lower, x1) + _dot(lower, x2) + _dot(lower, x3)
        o_ref[c:c + TK, :] = cs + carry
        carry = carry + jnp.sum(x, axis=0, keepdims=True)


def _cumsum_seq(lf, seq):
    n, h = lf.shape
    return pl.pallas_call(
        _cumsum_kernel,
        grid=(n // seq,),
        in_specs=[pl.BlockSpec((seq, h), lambda b: (b, 0))],
        out_specs=pl.BlockSpec((seq, h), lambda b: (b, 0)),
        out_shape=jax.ShapeDtypeStruct((n, h), F32),
        compiler_params=pltpu.CompilerParams(dimension_semantics=("arbitrary",)),
        name="gate_cumsum",
    )(lf)


def _stack_heads(q):
    lane = lax.broadcasted_iota(jnp.int32, q.shape, 1)
    zero = jnp.zeros_like(q)
    return jnp.concatenate([jnp.where(lane < HEAD_DIM, q, zero), jnp.where(lane >= HEAD_DIM, q, zero)], axis=0)


def _unstack_heads(acc):
    tq = acc.shape[0] // 2
    lane = lax.broadcasted_iota(jnp.int32, (tq, LANES), 1)
    return jnp.where(lane < HEAD_DIM, acc[:tq], acc[tq:])


def _diag_positions(qi):
    r = lax.broadcasted_iota(jnp.int32, (2 * TQ, TK), 0)
    c = lax.broadcasted_iota(jnp.int32, (2 * TQ, TK), 1)
    t = jnp.where(r >= TQ, r - TQ, r) + (qi % (TK // TQ)) * TQ
    return t, c


def _sb_attn_kernel(q_ref, k_ref, v_ref, tri_ref, o_ref):
    qi = pl.program_id(2)
    jd = qi // (TK // TQ)
    q2 = _stack_heads(q_ref[...] * (1.0 / math.sqrt(HEAD_DIM)))
    tri = tri_ref[...]

    def block(j, carry, acc, mask):
        kb = k_ref[pl.ds(pl.multiple_of(j * TK, TK), TK), :]
        vb = v_ref[pl.ds(pl.multiple_of(j * TK, TK), TK), :]
        z = _dot_nt(q2, kb)
        log_beta = jnp.minimum(z, 0.0) - jnp.log(1.0 + jnp.exp(-jnp.abs(z)))
        log_rem = log_beta - z
        if mask is not None:
            log_rem = jnp.where(mask, log_rem, 0.0)
        after = _dot(log_rem.astype(BF16), tri)
        w = jnp.exp(log_beta + after + carry)
        if mask is not None:
            w = jnp.where(mask, w, 0.0)
        acc = acc + _dot(w.astype(BF16), vb)
        carry = carry + jnp.sum(log_rem, axis=1, keepdims=True)
        return carry, acc

    t, c = _diag_positions(qi)
    carry, acc = block(jd, jnp.zeros((2 * TQ, 1), F32), jnp.zeros((2 * TQ, LANES), F32), c < t)

    def body(i, state):
        return block(jd - 1 - i, state[0], state[1], None)

    carry, acc = lax.fori_loop(0, jd, body, (carry, acc))
    o_ref[...] = _unstack_heads(acc).astype(o_ref.dtype)


def _fox_attn_kernel(q_ref, k_ref, v_ref, cum_ref, o_ref):
    qi = pl.program_id(2)
    jd = qi // (TK // TQ)
    q2 = _stack_heads(q_ref[...] * (1.0 / math.sqrt(HEAD_DIM)))

    def block(j, m, l, acc, mask):
        kb = k_ref[pl.ds(pl.multiple_of(j * TK, TK), TK), :]
        vb = v_ref[pl.ds(pl.multiple_of(j * TK, TK), TK), :]
        cs = cum_ref[j]
        bias = jnp.concatenate([jnp.broadcast_to(cs[0:1], (TQ, TK)), jnp.broadcast_to(cs[1:2], (TQ, TK))], axis=0)
        s = _dot_nt(q2, kb) - bias
        if mask is not None:
            s = jnp.where(mask, s, -jnp.inf)
        m_new = jnp.maximum(m, jnp.max(s, axis=1, keepdims=True))
        alpha = jnp.exp(m - m_new)
        p = jnp.exp(s - m_new)
        l = alpha * l + jnp.sum(p, axis=1, keepdims=True)
        acc = alpha * acc + _dot(p.astype(BF16), vb)
        return m_new, l, acc

    t, c = _diag_positions(qi)
    m, l, acc = block(jd, jnp.full((2 * TQ, 1), -jnp.inf, F32), jnp.zeros((2 * TQ, 1), F32),
                      jnp.zeros((2 * TQ, LANES), F32), c <= t)

    def body(i, state):
        return block(jd - 1 - i, state[0], state[1], state[2], None)

    m, l, acc = lax.fori_loop(0, jd, body, (m, l, acc))
    o_ref[...] = _unstack_heads(acc / l).astype(o_ref.dtype)


def _attention(kernel_fn, proj, batch, seq, width, extra, extra_spec, name):
    n_pairs = width // LANES
    proj3 = proj.reshape(batch, seq, proj.shape[1])
    return pl.pallas_call(
        kernel_fn,
        grid=(batch, n_pairs, seq // TQ),
        in_specs=[pl.BlockSpec((None, TQ, LANES), lambda b, p, i: (b, i, p)),
                  pl.BlockSpec((None, seq, LANES), lambda b, p, i: (b, 0, n_pairs + p)),
                  pl.BlockSpec((None, seq, LANES), lambda b, p, i: (b, 0, 2 * n_pairs + p)),
                  extra_spec],
        out_specs=pl.BlockSpec((None, TQ, LANES), lambda b, p, i: (b, i, p)),
        out_shape=jax.ShapeDtypeStruct((batch, seq, width), BF16),
        compiler_params=pltpu.CompilerParams(dimension_semantics=("arbitrary",) * 3,
                                             vmem_limit_bytes=VMEM_LIMIT),
        name=name,
    )(proj3, proj3, proj3, extra).reshape(batch * seq, width)


def _gated_out_kernel(o_ref, g_ref, x_ref, w_ref, fg_ref, out_ref, *, final_norm):
    og = (o_ref[...].astype(F32) * _silu(g_ref[...].astype(F32))).astype(BF16)
    y = x_ref[...] + _dot(og, w_ref[...])
    if final_norm:
        y = _rms(y, fg_ref[...])
    out_ref[...] = y


def _gated_out(o, proj, g_block, x2, w, final_gain, final_norm, tm=512):
    n, d = x2.shape
    e = o.shape[1]
    return pl.pallas_call(
        functools.partial(_gated_out_kernel, final_norm=final_norm),
        grid=(n // tm,),
        in_specs=[pl.BlockSpec((tm, e), lambda i: (i, 0)),
                  pl.BlockSpec((tm, e), lambda i: (i, g_block)),
                  pl.BlockSpec((tm, d), lambda i: (i, 0)),
                  _resident((e, d)), _resident((1, d))],
        out_specs=pl.BlockSpec((tm, d), lambda i: (i, 0)),
        out_shape=jax.ShapeDtypeStruct((n, d), F32),
        compiler_params=pltpu.CompilerParams(dimension_semantics=("arbitrary",),
                                             vmem_limit_bytes=VMEM_LIMIT),
        name="gated_out",
    )(o, proj, x2, w, final_gain)


def _gmlp_kernel(x_ref, gain_ref, win_ref, lng_ref, lnb_ref, ws_ref, bias_ref, wout_ref, out_ref,
                 v_scr, vn_scr, s_scr, og_scr, *, n_chunk):
    tm = x_ref.shape[0]
    width = wout_ref.shape[0]
    gw = width // GM_GROUPS
    hn = _rms(x_ref[...], gain_ref[...]).astype(BF16)

    for c in range(0, width, n_chunk):
        v_scr[:, c:c + n_chunk] = _gelu_tanh(_dot(hn, win_ref[:, width + c:width + c + n_chunk]))
    v = v_scr[...]
    mu = jnp.mean(v, axis=-1, keepdims=True)
    vc = v - mu
    var = jnp.mean(vc * vc, axis=-1, keepdims=True)
    vn_scr[...] = (vc * lax.rsqrt(var + EPS) * lng_ref[...] + lnb_ref[...]).astype(BF16)

    row = lax.broadcasted_iota(jnp.int32, (GM_CHUNK, GM_CHUNK), 0)
    col = lax.broadcasted_iota(jnp.int32, (GM_CHUNK, GM_CHUNK), 1)
    for g in range(GM_GROUPS):
        wsm = jnp.where(row >= col, ws_ref[g], 0.0).astype(BF16)
        for n in range(0, tm, GM_CHUNK):
            s_scr[n:n + GM_CHUNK, g * gw:(g + 1) * gw] = (
                _dot(wsm, vn_scr[n:n + GM_CHUNK, g * gw:(g + 1) * gw]) + bias_ref[:, g * gw:(g + 1) * gw])

    for c in range(0, width, n_chunk):
        u = _gelu_tanh(_dot(hn, win_ref[:, c:c + n_chunk]))
        gate = _dot(hn, win_ref[:, 2 * width + c:2 * width + c + n_chunk])
        og_scr[:, c:c + n_chunk] = ((u * s_scr[:, c:c + n_chunk]) * _silu(gate)).astype(BF16)

    out_ref[...] = x_ref[...] + _dot(og_scr[...], wout_ref[...])


def _gmlp_layer(x2, gain, win, lng, lnb, ws, bias_full, wout, tm=256, n_chunk=512):
    n, d = x2.shape
    width = wout.shape[0]
    return pl.pallas_call(
        functools.partial(_gmlp_kernel, n_chunk=n_chunk),
        grid=(n // tm,),
        in_specs=[pl.BlockSpec((tm, d), lambda i: (i, 0)), _resident((1, d)), _resident(win.shape),
                  _resident((1, width)), _resident((1, width)), _resident(ws.shape),
                  _resident(bias_full.shape), _resident(wout.shape)],
        out_specs=pl.BlockSpec((tm, d), lambda i: (i, 0)),
        out_shape=jax.ShapeDtypeStruct((n, d), F32),
        scratch_shapes=[pltpu.VMEM((tm, width), F32), pltpu.VMEM((tm, width), BF16),
                        pltpu.VMEM((tm, width), F32), pltpu.VMEM((tm, width), BF16)],
        compiler_params=pltpu.CompilerParams(dimension_semantics=("arbitrary",),
                                             vmem_limit_bytes=VMEM_LIMIT),
        name="gmlp_layer",
    )(x2, gain, win, lng, lnb, ws, bias_full, wout)


def _conv_kernel(x_ref, gain_ref, win_ref, cw_ref, cb_ref, lng_ref, lnb_ref, wout_ref, out_ref,
                 y_scr, c_scr, og_scr, *, n_chunk):
    tm = x_ref.shape[0]
    width = wout_ref.shape[0]
    n_lane_chunks = width // LANES
    t_idx = pl.program_id(1)
    hn = _rms(x_ref[...], gain_ref[...]).astype(BF16)

    @pl.when(t_idx == 0)
    def _():
        y_scr[:, 0:HALO, :] = jnp.zeros((n_lane_chunks, HALO, LANES), F32)

    for c in range(0, width, n_chunk):
        a = _dot(hn, win_ref[:, c:c + n_chunk])
        b = _dot(hn, win_ref[:, width + c:width + c + n_chunk])
        y = a * _sigmoid(b)
        for i in range(n_chunk // LANES):
            y_scr[c // LANES + i, HALO:HALO + tm, :] = y[:, i * LANES:(i + 1) * LANES]

    def conv_chunk(ci, _):
        acc = jnp.zeros((tm, LANES), F32)
        for k in range(CV_KERNEL):
            acc = acc + cw_ref[ci, pl.ds(k, 1), :] * y_scr[ci, pl.ds(HALO - (CV_KERNEL - 1) + k, tm), :]
        c_scr[ci] = acc + cb_ref[ci]
        return 0

    lax.fori_loop(0, n_lane_chunks, conv_chunk, 0)

    for ci in range(n_lane_chunks):
        y_scr[ci, 0:HALO, :] = y_scr[ci, tm:tm + HALO, :]

    tot = c_scr[0]
    for ci in range(1, n_lane_chunks):
        tot = tot + c_scr[ci]
    mu = jnp.sum(tot, axis=-1, keepdims=True) * (1.0 / width)
    d0 = c_scr[0] - mu
    sq = d0 * d0
    for ci in range(1, n_lane_chunks):
        dc = c_scr[ci] - mu
        sq = sq + dc * dc
    rstd = lax.rsqrt(jnp.sum(sq, axis=-1, keepdims=True) * (1.0 / width) + EPS)

    for c in range(0, width, n_chunk):
        gate = _dot(hn, win_ref[:, 2 * width + c:2 * width + c + n_chunk])
        for i in range(n_chunk // LANES):
            ci = c // LANES + i
            lo = ci * LANES
            yn = (c_scr[ci] - mu) * rstd * lng_ref[:, lo:lo + LANES] + lnb_ref[:, lo:lo + LANES]
            og_scr[:, lo:lo + LANES] = (_silu(yn) * _silu(gate[:, i * LANES:(i + 1) * LANES])).astype(BF16)

    out_ref[...] = x_ref[...] + _dot(og_scr[...], wout_ref[...])


def _conv_layer(x3, gain, win, cw, cb, lng, lnb, wout, tm=256, n_chunk=512):
    batch, seq, d = x3.shape
    width = wout.shape[0]
    n_lane_chunks = width // LANES
    return pl.pallas_call(
        functools.partial(_conv_kernel, n_chunk=n_chunk),
        grid=(batch, seq // tm),
        in_specs=[pl.BlockSpec((None, tm, d), lambda b, t: (b, t, 0)), _resident((1, d)), _resident(win.shape),
                  _resident(cw.shape), _resident(cb.shape), _resident((1, width)), _resident((1, width)),
                  _resident(wout.shape)],
        out_specs=pl.BlockSpec((None, tm, d), lambda b, t: (b, t, 0)),
        out_shape=jax.ShapeDtypeStruct((batch, seq, d), F32),
        scratch_shapes=[pltpu.VMEM((n_lane_chunks, tm + HALO, LANES), F32),
                        pltpu.VMEM((n_lane_chunks, tm, LANES), F32),
                        pltpu.VMEM((tm, width), BF16)],
        compiler_params=pltpu.CompilerParams(dimension_semantics=("arbitrary", "arbitrary"),
                                             vmem_limit_bytes=VMEM_LIMIT),
        name="conv_layer",
    )(x3, gain, win, cw, cb, lng, lnb, wout)


def _layer_stick_breaking(x2, batch, seq, gain, w_in, w_out, final_gain, final_norm):
    width = w_out.shape[0]
    proj = _norm_proj(x2, gain[None, :], w_in.astype(BF16))
    row = lax.broadcasted_iota(jnp.int32, (TK, TK), 0)
    col = lax.broadcasted_iota(jnp.int32, (TK, TK), 1)
    tri = (row > col).astype(BF16)
    o = _attention(_sb_attn_kernel, proj, batch, seq, width, tri, _resident((TK, TK)), "sb_attention")
    return _gated_out(o, proj, 3, x2, w_out.astype(BF16), final_gain, final_norm)


def _layer_forgetting(x2, batch, seq, gain, w_in, b_f, w_out, final_gain, final_norm):
    width = w_out.shape[0]
    n_heads = b_f.shape[0]
    d = w_in.shape[0]
    w_main = w_in[:, :4 * width].astype(BF16)
    w_f = jnp.zeros((d, LANES), BF16).at[:, :n_heads].set(w_in[:, 4 * width:].astype(BF16))
    b_pad = jnp.zeros((1, LANES), F32).at[0, :n_heads].set(b_f)
    proj, lf = _norm_proj_gate(x2, gain[None, :], w_main, w_f, b_pad)
    cum = _cumsum_seq(lf, seq)
    n_pairs = n_heads // 2
    cum = cum[:, :n_heads].reshape(batch, seq // TK, TK, n_pairs, 2).transpose(0, 3, 1, 4, 2)
    cum_spec = pl.BlockSpec((None, None, seq // TK, 2, TK), lambda b, p, i: (b, p, 0, 0, 0))
    o = _attention(_fox_attn_kernel, proj, batch, seq, width, cum, cum_spec, "fox_attention")
    return _gated_out(o, proj, 3, x2, w_out.astype(BF16), final_gain, final_norm)


def _layer_gmlp(x2, gain, w_in, ln_g, ln_b, w_s, b_s, w_out):
    width = w_out.shape[0]
    bias_full = jnp.repeat(b_s.T, width // GM_GROUPS, axis=1)
    return _gmlp_layer(x2, gain[None, :], w_in.astype(BF16), ln_g[None, :], ln_b[None, :], w_s, bias_full,
                       w_out.astype(BF16))


def _layer_conv(x2, batch, seq, gain, w_in, conv_w, conv_b, ln_g, ln_b, w_out):
    width = w_out.shape[0]
    n_lane_chunks = width // LANES
    d = x2.shape[1]
    cw = jnp.zeros((HALO, width), F32).at[:CV_KERNEL].set(conv_w)
    cw = cw.reshape(HALO, n_lane_chunks, LANES).transpose(1, 0, 2)
    cb = conv_b.reshape(n_lane_chunks, 1, LANES)
    y = _conv_layer(x2.reshape(batch, seq, d), gain[None, :], w_in.astype(BF16), cw, cb, ln_g[None, :],
                    ln_b[None, :], w_out.astype(BF16))
    return y.reshape(batch * seq, d)


def kernel(x, a_norm, a_w_in, a_w_out, b_norm, b_w_in, b_v_ln_g, b_v_ln_b, b_w_s, b_b_s, b_w_out, c_norm, c_w_in, c_conv_w, c_conv_b, c_ln_g, c_ln_b, c_w_out, d_norm, d_w_in, d_b_f, d_w_out, final_norm):
    batch, seq, d = x.shape
    counts = (a_norm.shape[0], b_norm.shape[0], c_norm.shape[0], d_norm.shape[0])
    depth = sum(counts)
    fg = final_norm[None, :]
    x2 = x.reshape(batch * seq, d)
    for i in range(depth):
        kind, j = i % 4, i // 4
        last = i == depth - 1
        if kind == 0:
            x2 = _layer_stick_breaking(x2, batch, seq, a_norm[j], a_w_in[j], a_w_out[j], fg, last)
        elif kind == 1:
            x2 = _layer_gmlp(x2, b_norm[j], b_w_in[j], b_v_ln_g[j], b_v_ln_b[j], b_w_s[j], b_b_s[j], b_w_out[j])
        elif kind == 2:
            x2 = _layer_conv(x2, batch, seq, c_norm[j], c_w_in[j], c_conv_w[j], c_conv_b[j], c_ln_g[j],
                             c_ln_b[j], c_w_out[j])
        else:
            x2 = _layer_forgetting(x2, batch, seq, d_norm[j], d_w_in[j], d_b_f[j], d_w_out[j], fg, last)
    return x2.reshape(batch, seq, d)
```

```python
import functools
import math

import jax
import jax.numpy as jnp
from jax import lax
from jax.experimental import pallas as pl
from jax.experimental.pallas import tpu as pltpu

EPS = 1e-6
LANES = 128
HEAD_DIM = 64
GM_CHUNK = 128
GM_GROUPS = 16
CV_KERNEL = 31
HALO = 32
TQ = 256
TK = 256
PAIRS = 8
VMEM_LIMIT = 56 * 1024 * 1024
LOG2E = math.log2(math.e)
Q_SCALE = LOG2E / math.sqrt(HEAD_DIM)

F32 = jnp.float32
BF16 = jnp.bfloat16


def _resident(shape):
    nd = len(shape)
    return pl.BlockSpec(shape, lambda *_: (0,) * nd, pipeline_mode=pl.Buffered(1))


def _rms(x, gain):
    return x * lax.rsqrt(jnp.mean(x * x, axis=-1, keepdims=True) + EPS) * gain


def _sigmoid(x):
    return 1.0 / (1.0 + jnp.exp(-x))


def _silu(x):
    return x * _sigmoid(x)


def _gelu_tanh(x):
    c = math.sqrt(2.0 / math.pi)
    return 0.5 * x * (1.0 + jnp.tanh(c * (x + 0.044715 * (x * x * x))))


def _log_sigmoid(x):
    return jnp.minimum(x, 0.0) - jnp.log(1.0 + jnp.exp(-jnp.abs(x)))


def _dot(a, b):
    return jnp.dot(a, b, preferred_element_type=F32)


def _dot_nt(a, b):
    return lax.dot_general(a, b, (((1,), (1,)), ((), ())), preferred_element_type=F32)


def _norm_proj_kernel(x_ref, gain_ref, w_ref, o_ref, *, n_chunk):
    hn = _rms(x_ref[...], gain_ref[...]).astype(BF16)
    for c in range(0, o_ref.shape[1], n_chunk):
        o_ref[:, c:c + n_chunk] = _dot(hn, w_ref[:, c:c + n_chunk]).astype(o_ref.dtype)


def _norm_proj_gate_kernel(x_ref, gain_ref, w_ref, wf_ref, bf_ref, o_ref, lf_ref, *, n_chunk):
    hn = _rms(x_ref[...], gain_ref[...]).astype(BF16)
    for c in range(0, o_ref.shape[1], n_chunk):
        o_ref[:, c:c + n_chunk] = _dot(hn, w_ref[:, c:c + n_chunk]).astype(o_ref.dtype)
    lf_ref[...] = _log_sigmoid(_dot(hn, wf_ref[...]) + bf_ref[...]) * LOG2E


def _norm_proj(x2, gain, w, tm=512, n_chunk=1024):
    n, d = x2.shape
    e = w.shape[1]
    return pl.pallas_call(
        functools.partial(_norm_proj_kernel, n_chunk=n_chunk),
        grid=(n // tm,),
        in_specs=[pl.BlockSpec((tm, d), lambda i: (i, 0)), _resident((1, d)), _resident((d, e))],
        out_specs=pl.BlockSpec((tm, e), lambda i: (i, 0)),
        out_shape=jax.ShapeDtypeStruct((n, e), BF16),
        compiler_params=pltpu.CompilerParams(dimension_semantics=("arbitrary",),
                                             vmem_limit_bytes=VMEM_LIMIT),
        name="norm_proj",
    )(x2, gain, w)


def _norm_proj_gate(x2, gain, w, wf, bf, tm=512, n_chunk=1024):
    n, d = x2.shape
    e = w.shape[1]
    return pl.pallas_call(
        functools.partial(_norm_proj_gate_kernel, n_chunk=n_chunk),
        grid=(n // tm,),
        in_specs=[pl.BlockSpec((tm, d), lambda i: (i, 0)), _resident((1, d)), _resident((d, e)),
                  _resident((d, LANES)), _resident((1, LANES))],
        out_specs=[pl.BlockSpec((tm, e), lambda i: (i, 0)),
                   pl.BlockSpec((tm, LANES), lambda i: (i, 0))],
        out_shape=[jax.ShapeDtypeStruct((n, e), BF16), jax.ShapeDtypeStruct((n, LANES), F32)],
        compiler_params=pltpu.CompilerParams(dimension_semantics=("arbitrary",),
                                             vmem_limit_bytes=VMEM_LIMIT),
        name="norm_proj_gate",
    )(x2, gain, w, wf, bf)


def _split3(x):
    x1 = x.astype(BF16)
    r1 = x - x1.astype(F32)
    x2 = r1.astype(BF16)
    x3 = (r1 - x2.astype(F32)).astype(BF16)
    return x1, x2, x3


def _cumsum_kernel(lf_ref, o_ref):
    s, h = lf_ref.shape
    row = lax.broadcasted_iota(jnp.int32, (TK, TK), 0)
    col = lax.broadcasted_iota(jnp.int32, (TK, TK), 1)
    lower = jnp.where(col <= row, 1.0, 0.0).astype(BF16)
    carry = jnp.zeros((1, h), F32)
    for c in range(0, s, TK):
        x = lf_ref[c:c + TK, :]
        x1, x2, x3 = _split3(x)
        cs = _dot(lower, x1) + _dot(lower, x2) + _dot(lower, x3)
        o_ref[c:c + TK, :] = cs + carry
        carry = carry + jnp.sum(x, axis=0, keepdims=True)


def _cumsum_seq(lf, seq):
    n, h = lf.shape
    return pl.pallas_call(
        _cumsum_kernel,
        grid=(n // seq,),
        in_specs=[pl.BlockSpec((seq, h), lambda b: (b, 0))],
        out_specs=pl.BlockSpec((seq, h), lambda b: (b, 0)),
        out_shape=jax.ShapeDtypeStruct((n, h), F32),
        compiler_params=pltpu.CompilerParams(dimension_semantics=("arbitrary",)),
        name="gate_cumsum",
    )(lf)


def _stack_heads(q):
    lane = lax.broadcasted_iota(jnp.int32, q.shape, 1)
    zero = jnp.zeros_like(q)
    return jnp.concatenate([jnp.where(lane < HEAD_DIM, q, zero), jnp.where(lane >= HEAD_DIM, q, zero)], axis=0)


def _unstack_heads(acc):
    tq = acc.shape[0] // 2
    lane = lax.broadcasted_iota(jnp.int32, (tq, LANES), 1)
    return jnp.where(lane < HEAD_DIM, acc[:tq], acc[tq:])


def _diag_mask(strict):
    r = lax.broadcasted_iota(jnp.int32, (2 * TQ, TK), 0)
    c = lax.broadcasted_iota(jnp.int32, (2 * TQ, TK), 1)
    t = jnp.where(r >= TQ, r - TQ, r)
    return c < t if strict else c <= t


def _pair(ref, g, rows):
    return ref[rows, g * LANES:(g + 1) * LANES]


def _halves(x):
    return x[:, :LANES], x[:, LANES:]


def _rowsum(x0, x1):
    return jnp.broadcast_to(jnp.sum(x0 + x1, axis=1, keepdims=True), x0.shape)


def _neg_abs(x):
    bits = lax.bitcast_convert_type(x, jnp.uint32) | jnp.uint32(0x80000000)
    return lax.bitcast_convert_type(bits, F32)


def _sb_attn_kernel(q_ref, k_ref, v_ref, tri_ref, o_ref, q2_scr, carry_scr, acc_scr):
    qi = pl.program_id(2)
    for g in range(PAIRS):
        q2_scr[g] = _stack_heads(_pair(q_ref, g, slice(None)))
        carry_scr[g] = jnp.zeros((2 * TQ, LANES), F32)
        acc_scr[g] = jnp.zeros((2 * TQ, LANES), F32)

    def block(j, mask):
        rows = pl.ds(pl.multiple_of(j * TK, TK), TK)
        for g in range(PAIRS):
            z = _dot_nt(q2_scr[g], _pair(k_ref, g, rows))
            log_beta = jnp.minimum(z, 0.0) - jnp.log2(1.0 + jnp.exp2(_neg_abs(z)))
            log_rem = log_beta - z
            if mask is not None:
                log_rem = jnp.where(mask, log_rem, 0.0)
            after = _dot(log_rem.astype(BF16), tri_ref[...])
            carry = carry_scr[g]
            t0, t1 = _halves(log_beta + after)
            w = jnp.exp2(jnp.concatenate([t0 + carry, t1 + carry], axis=1))
            if mask is not None:
                w = jnp.where(mask, w, 0.0)
            acc_scr[g] += _dot(w.astype(BF16), _pair(v_ref, g, rows))
            carry_scr[g] = carry + _rowsum(*_halves(log_rem))

    block(qi, _diag_mask(strict=True))

    def body(i, _):
        block(qi - 1 - i, None)
        return 0

    lax.fori_loop(0, qi, body, 0)
    for g in range(PAIRS):
        o_ref[:, g * LANES:(g + 1) * LANES] = _unstack_heads(acc_scr[g]).astype(o_ref.dtype)


def _fox_attn_kernel(q_ref, k_ref, v_ref, cum_ref, o_ref, q2_scr, m_scr, l_scr, acc_scr):
    qi = pl.program_id(2)
    for g in range(PAIRS):
        q2_scr[g] = _stack_heads(_pair(q_ref, g, slice(None)))
        m_scr[g] = jnp.full((2 * TQ, LANES), -jnp.inf, F32)
        l_scr[g] = jnp.zeros((2 * TQ, LANES), F32)
        acc_scr[g] = jnp.zeros((2 * TQ, LANES), F32)

    def block(j, mask):
        rows = pl.ds(pl.multiple_of(j * TK, TK), TK)
        for g in range(PAIRS):
            cs = cum_ref[g, j]
            bias = jnp.concatenate([jnp.broadcast_to(cs[0:1], (TQ, TK)), jnp.broadcast_to(cs[1:2], (TQ, TK))],
                                   axis=0)
            s = _dot_nt(q2_scr[g], _pair(k_ref, g, rows)) - bias
            if mask is not None:
                s = jnp.where(mask, s, -jnp.inf)
            s0, s1 = _halves(s)
            m_old = m_scr[g]
            m_new = jnp.maximum(m_old, jnp.max(jnp.maximum(s0, s1), axis=1, keepdims=True))
            alpha = jnp.exp2(m_old - m_new)
            p0 = jnp.exp2(s0 - m_new)
            p1 = jnp.exp2(s1 - m_new)
            l_scr[g] = alpha * l_scr[g] + _rowsum(p0, p1)
            pv = _dot(jnp.concatenate([p0, p1], axis=1).astype(BF16), _pair(v_ref, g, rows))
            acc_scr[g] = alpha * acc_scr[g] + pv
            m_scr[g] = m_new

    block(qi, _diag_mask(strict=False))

    def body(i, _):
        block(qi - 1 - i, None)
        return 0

    lax.fori_loop(0, qi, body, 0)
    for g in range(PAIRS):
        o_ref[:, g * LANES:(g + 1) * LANES] = _unstack_heads(acc_scr[g] / l_scr[g]).astype(o_ref.dtype)


def _attention(kernel_fn, n_stats, proj, batch, seq, width, extra, extra_spec, name):
    gw = PAIRS * LANES
    n_groups = width // gw
    proj3 = proj.reshape(batch, seq, proj.shape[1])
    return pl.pallas_call(
        kernel_fn,
        grid=(batch, n_groups, seq // TQ),
        in_specs=[pl.BlockSpec((None, TQ, gw), lambda b, p, i: (b, i, p)),
                  pl.BlockSpec((None, seq, gw), lambda b, p, i: (b, 0, n_groups + p)),
                  pl.BlockSpec((None, seq, gw), lambda b, p, i: (b, 0, 2 * n_groups + p)),
                  extra_spec],
        out_specs=pl.BlockSpec((None, TQ, gw), lambda b, p, i: (b, i, p)),
        out_shape=jax.ShapeDtypeStruct((batch, seq, width), BF16),
        scratch_shapes=[pltpu.VMEM((PAIRS, 2 * TQ, LANES), BF16)]
        + [pltpu.VMEM((PAIRS, 2 * TQ, LANES), F32)] * n_stats,
        compiler_params=pltpu.CompilerParams(dimension_semantics=("arbitrary",) * 3,
                                             vmem_limit_bytes=VMEM_LIMIT),
        name=name,
    )(proj3, proj3, proj3, extra).reshape(batch * seq, width)


def _gated_out_kernel(o_ref, g_ref, x_ref, w_ref, fg_ref, out_ref, *, final_norm):
    og = (o_ref[...].astype(F32) * _silu(g_ref[...].astype(F32))).astype(BF16)
    y = x_ref[...] + _dot(og, w_ref[...])
    if final_norm:
        y = _rms(y, fg_ref[...])
    out_ref[...] = y


def _gated_out(o, proj, g_block, x2, w, final_gain, final_norm, tm=512):
    n, d = x2.shape
    e = o.shape[1]
    return pl.pallas_call(
        functools.partial(_gated_out_kernel, final_norm=final_norm),
        grid=(n // tm,),
        in_specs=[pl.BlockSpec((tm, e), lambda i: (i, 0)),
                  pl.BlockSpec((tm, e), lambda i: (i, g_block)),
                  pl.BlockSpec((tm, d), lambda i: (i, 0)),
                  _resident((e, d)), _resident((1, d))],
        out_specs=pl.BlockSpec((tm, d), lambda i: (i, 0)),
        out_shape=jax.ShapeDtypeStruct((n, d), F32),
        compiler_params=pltpu.CompilerParams(dimension_semantics=("arbitrary",),
                                             vmem_limit_bytes=VMEM_LIMIT),
        name="gated_out",
    )(o, proj, x2, w, final_gain)


def _gmlp_kernel(x_ref, gain_ref, win_ref, lng_ref, lnb_ref, ws_ref, bias_ref, wout_ref, out_ref,
                 v_scr, vn_scr, s_scr, og_scr, *, n_chunk):
    tm = x_ref.shape[0]
    width = wout_ref.shape[0]
    gw = width // GM_GROUPS
    hn = _rms(x_ref[...], gain_ref[...]).astype(BF16)

    for c in range(0, width, n_chunk):
        v_scr[:, c:c + n_chunk] = _gelu_tanh(_dot(hn, win_ref[:, width + c:width + c + n_chunk]))
    v = v_scr[...]
    mu = jnp.mean(v, axis=-1, keepdims=True)
    vc = v - mu
    var = jnp.mean(vc * vc, axis=-1, keepdims=True)
    vn_scr[...] = (vc * lax.rsqrt(var + EPS) * lng_ref[...] + lnb_ref[...]).astype(BF16)

    row = lax.broadcasted_iota(jnp.int32, (GM_CHUNK, GM_CHUNK), 0)
    col = lax.broadcasted_iota(jnp.int32, (GM_CHUNK, GM_CHUNK), 1)
    for g in range(GM_GROUPS):
        wsm = jnp.where(row >= col, ws_ref[g], 0.0).astype(BF16)
        for n in range(0, tm, GM_CHUNK):
            s_scr[n:n + GM_CHUNK, g * gw:(g + 1) * gw] = (
                _dot(wsm, vn_scr[n:n + GM_CHUNK, g * gw:(g + 1) * gw]) + bias_ref[:, g * gw:(g + 1) * gw])

    for c in range(0, width, n_chunk):
        u = _gelu_tanh(_dot(hn, win_ref[:, c:c + n_chunk]))
        gate = _dot(hn, win_ref[:, 2 * width + c:2 * width + c + n_chunk])
        og_scr[:, c:c + n_chunk] = ((u * s_scr[:, c:c + n_chunk]) * _silu(gate)).astype(BF16)

    out_ref[...] = x_ref[...] + _dot(og_scr[...], wout_ref[...])


def _gmlp_layer(x2, gain, win, lng, lnb, ws, bias_full, wout, tm=256, n_chunk=512):
    n, d = x2.shape
    width = wout.shape[0]
    return pl.pallas_call(
        functools.partial(_gmlp_kernel, n_chunk=n_chunk),
        grid=(n // tm,),
        in_specs=[pl.BlockSpec((tm, d), lambda i: (i, 0)), _resident((1, d)), _resident(win.shape),
                  _resident((1, width)), _resident((1, width)), _resident(ws.shape),
                  _resident(bias_full.shape), _resident(wout.shape)],
        out_specs=pl.BlockSpec((tm, d), lambda i: (i, 0)),
        out_shape=jax.ShapeDtypeStruct((n, d), F32),
        scratch_shapes=[pltpu.VMEM((tm, width), F32), pltpu.VMEM((tm, width), BF16),
                        pltpu.VMEM((tm, width), F32), pltpu.VMEM((tm, width), BF16)],
        compiler_params=pltpu.CompilerParams(dimension_semantics=("arbitrary",),
                                             vmem_limit_bytes=VMEM_LIMIT),
        name="gmlp_layer",
    )(x2, gain, win, lng, lnb, ws, bias_full, wout)


def _conv_kernel(x_ref, gain_ref, win_ref, cw_ref, cb_ref, lng_ref, lnb_ref, wout_ref, out_ref,
                 y_scr, c_scr, og_scr, *, n_chunk):
    tm = x_ref.shape[0]
    width = wout_ref.shape[0]
    n_lane_chunks = width // LANES
    t_idx = pl.program_id(1)
    hn = _rms(x_ref[...], gain_ref[...]).astype(BF16)

    @pl.when(t_idx == 0)
    def _():
        y_scr[:, 0:HALO, :] = jnp.zeros((n_lane_chunks, HALO, LANES), F32)

    for c in range(0, width, n_chunk):
        a = _dot(hn, win_ref[:, c:c + n_chunk])
        b = _dot(hn, win_ref[:, width + c:width + c + n_chunk])
        y = a * _sigmoid(b)
        for i in range(n_chunk // LANES):
            y_scr[c // LANES + i, HALO:HALO + tm, :] = y[:, i * LANES:(i + 1) * LANES]

    def conv_chunk(ci, _):
        acc = jnp.zeros((tm, LANES), F32)
        for k in range(CV_KERNEL):
            acc = acc + cw_ref[ci, pl.ds(k, 1), :] * y_scr[ci, pl.ds(HALO - (CV_KERNEL - 1) + k, tm), :]
        c_scr[ci] = acc + cb_ref[ci]
        return 0

    lax.fori_loop(0, n_lane_chunks, conv_chunk, 0)

    for ci in range(n_lane_chunks):
        y_scr[ci, 0:HALO, :] = y_scr[ci, tm:tm + HALO, :]

    tot = c_scr[0]
    for ci in range(1, n_lane_chunks):
        tot = tot + c_scr[ci]
    mu = jnp.sum(tot, axis=-1, keepdims=True) * (1.0 / width)
    d0 = c_scr[0] - mu
    sq = d0 * d0
    for ci in range(1, n_lane_chunks):
        dc = c_scr[ci] - mu
        sq = sq + dc * dc
    rstd = lax.rsqrt(jnp.sum(sq, axis=-1, keepdims=True) * (1.0 / width) + EPS)

    for c in range(0, width, n_chunk):
        gate = _dot(hn, win_ref[:, 2 * width + c:2 * width + c + n_chunk])
        for i in range(n_chunk // LANES):
            ci = c // LANES + i
            lo = ci * LANES
            yn = (c_scr[ci] - mu) * rstd * lng_ref[:, lo:lo + LANES] + lnb_ref[:, lo:lo + LANES]
            og_scr[:, lo:lo + LANES] = (_silu(yn) * _silu(gate[:, i * LANES:(i + 1) * LANES])).astype(BF16)

    out_ref[...] = x_ref[...] + _dot(og_scr[...], wout_ref[...])


def _conv_layer(x3, gain, win, cw, cb, lng, lnb, wout, tm=256, n_chunk=512):
    batch, seq, d = x3.shape
    width = wout.shape[0]
    n_lane_chunks = width // LANES
    return pl.pallas_call(
        functools.partial(_conv_kernel, n_chunk=n_chunk),
        grid=(batch, seq // tm),
        in_specs=[pl.BlockSpec((None, tm, d), lambda b, t: (b, t, 0)), _resident((1, d)), _resident(win.shape),
                  _resident(cw.shape), _resident(cb.shape), _resident((1, width)), _resident((1, width)),
                  _resident(wout.shape)],
        out_specs=pl.BlockSpec((None, tm, d), lambda b, t: (b, t, 0)),
        out_shape=jax.ShapeDtypeStruct((batch, seq, d), F32),
        scratch_shapes=[pltpu.VMEM((n_lane_chunks, tm + HALO, LANES), F32),
                        pltpu.VMEM((n_lane_chunks, tm, LANES), F32),
                        pltpu.VMEM((tm, width), BF16)],
        compiler_params=pltpu.CompilerParams(dimension_semantics=("arbitrary", "arbitrary"),
                                             vmem_limit_bytes=VMEM_LIMIT),
        name="conv_layer",
    )(x3, gain, win, cw, cb, lng, lnb, wout)


def _scale_q_columns(w_in, width):
    col = lax.broadcasted_iota(jnp.int32, (1, w_in.shape[1]), 1)
    return w_in * jnp.where(col < width, Q_SCALE, 1.0)


def _layer_stick_breaking(x2, batch, seq, gain, w_in, w_out, final_gain, final_norm):
    width = w_out.shape[0]
    proj = _norm_proj(x2, gain[None, :], _scale_q_columns(w_in, width).astype(BF16))
    row = lax.broadcasted_iota(jnp.int32, (TK, TK), 0)
    col = lax.broadcasted_iota(jnp.int32, (TK, TK), 1)
    tri = (row > col).astype(BF16)
    o = _attention(_sb_attn_kernel, 2, proj, batch, seq, width, tri, _resident((TK, TK)), "sb_attention")
    return _gated_out(o, proj, 3, x2, w_out.astype(BF16), final_gain, final_norm)


def _layer_forgetting(x2, batch, seq, gain, w_in, b_f, w_out, final_gain, final_norm):
    width = w_out.shape[0]
    n_heads = b_f.shape[0]
    d = w_in.shape[0]
    w_main = _scale_q_columns(w_in[:, :4 * width], width).astype(BF16)
    w_f = jnp.zeros((d, LANES), BF16).at[:, :n_heads].set(w_in[:, 4 * width:].astype(BF16))
    b_pad = jnp.zeros((1, LANES), F32).at[0, :n_heads].set(b_f)
    proj, lf = _norm_proj_gate(x2, gain[None, :], w_main, w_f, b_pad)
    cum = _cumsum_seq(lf, seq)
    n_pairs = n_heads // 2
    cum = cum[:, :n_heads].reshape(batch, seq // TK, TK, n_pairs, 2).transpose(0, 3, 1, 4, 2)
    cum_spec = pl.BlockSpec((None, PAIRS, seq // TK, 2, TK), lambda b, p, i: (b, p, 0, 0, 0))
    o = _attention(_fox_attn_kernel, 3, proj, batch, seq, width, cum, cum_spec, "fox_attention")
    return _gated_out(o, proj, 3, x2, w_out.astype(BF16), final_gain, final_norm)


def _layer_gmlp(x2, gain, w_in, ln_g, ln_b, w_s, b_s, w_out):
    width = w_out.shape[0]
    bias_full = jnp.repeat(b_s.T, width // GM_GROUPS, axis=1)
    return _gmlp_layer(x2, gain[None, :], w_in.astype(BF16), ln_g[None, :], ln_b[None, :], w_s, bias_full,
                       w_out.astype(BF16))


def _layer_conv(x2, batch, seq, gain, w_in, conv_w, conv_b, ln_g, ln_b, w_out):
    width = w_out.shape[0]
    n_lane_chunks = width // LANES
    d = x2.shape[1]
    cw = jnp.zeros((HALO, width), F32).at[:CV_KERNEL].set(conv_w)
    cw = cw.reshape(HALO, n_lane_chunks, LANES).transpose(1, 0, 2)
    cb = conv_b.reshape(n_lane_chunks, 1, LANES)
    y = _conv_layer(x2.reshape(batch, seq, d), gain[None, :], w_in.astype(BF16), cw, cb, ln_g[None, :],
                    ln_b[None, :], w_out.astype(BF16))
    return y.reshape(batch * seq, d)


def kernel(x, a_norm, a_w_in, a_w_out, b_norm, b_w_in, b_v_ln_g, b_v_ln_b, b_w_s, b_b_s, b_w_out, c_norm, c_w_in, c_conv_w, c_conv_b, c_ln_g, c_ln_b, c_w_out, d_norm, d_w_in, d_b_f, d_w_out, final_norm):
    batch, seq, d = x.shape
    counts = (a_norm.shape[0], b_norm.shape[0], c_norm.shape[0], d_norm.shape[0])
    depth = sum(counts)
    fg = final_norm[None, :]
    x2 = x.reshape(batch * seq, d)
    for i in range(depth):
        kind, j = i % 4, i // 4
        last = i == depth - 1
        if kind == 0:
            x2 = _layer_stick_breaking(x2, batch, seq, a_norm[j], a_w_in[j], a_w_out[j], fg, last)
        elif kind == 1:
            x2 = _layer_gmlp(x2, b_norm[j], b_w_in[j], b_v_ln_g[j], b_v_ln_b[j], b_w_s[j], b_b_s[j], b_w_out[j])
        elif kind == 2:
            x2 = _layer_conv(x2, batch, seq, c_norm[j], c_w_in[j], c_conv_w[j], c_conv_b[j], c_ln_g[j],
                             c_ln_b[j], c_w_out[j])
        else:
            x2 = _layer_forgetting(x2, batch, seq, d_norm[j], d_w_in[j], d_b_f[j], d_w_out[j], fg, last)
    return x2.reshape(batch, seq, d)
```

```python
import functools
import math

import jax
import jax.numpy as jnp
from jax import lax
from jax.experimental import pallas as pl
from jax.experimental.pallas import tpu as pltpu

EPS = 1e-6
LANES = 128
HEAD_DIM = 64
GM_CHUNK = 128
GM_GROUPS = 16
CV_KERNEL = 31
HALO = 32
TQ = 256
TK = 256
PAIRS = 8
VMEM_LIMIT = 56 * 1024 * 1024
LOG2E = math.log2(math.e)
UNDERFLOW_LOG2 = -150.0
Q_SCALE = LOG2E / math.sqrt(HEAD_DIM)

F32 = jnp.float32
BF16 = jnp.bfloat16


def _resident(shape):
    nd = len(shape)
    return pl.BlockSpec(shape, lambda *_: (0,) * nd, pipeline_mode=pl.Buffered(1))


def _rms(x, gain):
    return x * lax.rsqrt(jnp.mean(x * x, axis=-1, keepdims=True) + EPS) * gain


def _sigmoid(x):
    return 0.5 + 0.5 * jnp.tanh(0.5 * x)


def _silu(x):
    h = 0.5 * x
    return h + h * jnp.tanh(h)


def _gelu_tanh(x):
    c = math.sqrt(2.0 / math.pi)
    return 0.5 * x * (1.0 + jnp.tanh(c * (x + 0.044715 * (x * x * x))))


def _log_sigmoid(x):
    return jnp.minimum(x, 0.0) - jnp.log(1.0 + jnp.exp(-jnp.abs(x)))


def _dot(a, b):
    return jnp.dot(a, b, preferred_element_type=F32)


def _dot_nt(a, b):
    return lax.dot_general(a, b, (((1,), (1,)), ((), ())), preferred_element_type=F32)


def _norm_proj_kernel(x_ref, gain_ref, w_ref, o_ref, *, n_chunk):
    hn = _rms(x_ref[...], gain_ref[...]).astype(BF16)
    for c in range(0, o_ref.shape[1], n_chunk):
        o_ref[:, c:c + n_chunk] = _dot(hn, w_ref[:, c:c + n_chunk]).astype(o_ref.dtype)


def _norm_proj_gate_kernel(x_ref, gain_ref, w_ref, wf_ref, bf_ref, o_ref, lf_ref, *, n_chunk):
    hn = _rms(x_ref[...], gain_ref[...]).astype(BF16)
    for c in range(0, o_ref.shape[1], n_chunk):
        o_ref[:, c:c + n_chunk] = _dot(hn, w_ref[:, c:c + n_chunk]).astype(o_ref.dtype)
    lf_ref[...] = _log_sigmoid(_dot(hn, wf_ref[...]) + bf_ref[...]) * LOG2E


def _norm_proj(x2, gain, w, tm=512, n_chunk=1024):
    n, d = x2.shape
    e = w.shape[1]
    return pl.pallas_call(
        functools.partial(_norm_proj_kernel, n_chunk=n_chunk),
        grid=(n // tm,),
        in_specs=[pl.BlockSpec((tm, d), lambda i: (i, 0)), _resident((1, d)), _resident((d, e))],
        out_specs=pl.BlockSpec((tm, e), lambda i: (i, 0)),
        out_shape=jax.ShapeDtypeStruct((n, e), BF16),
        compiler_params=pltpu.CompilerParams(dimension_semantics=("arbitrary",),
                                             vmem_limit_bytes=VMEM_LIMIT),
        name="norm_proj",
    )(x2, gain, w)


def _norm_proj_gate(x2, gain, w, wf, bf, tm=512, n_chunk=1024):
    n, d = x2.shape
    e = w.shape[1]
    return pl.pallas_call(
        functools.partial(_norm_proj_gate_kernel, n_chunk=n_chunk),
        grid=(n // tm,),
        in_specs=[pl.BlockSpec((tm, d), lambda i: (i, 0)), _resident((1, d)), _resident((d, e)),
                  _resident((d, LANES)), _resident((1, LANES))],
        out_specs=[pl.BlockSpec((tm, e), lambda i: (i, 0)),
                   pl.BlockSpec((tm, LANES), lambda i: (i, 0))],
        out_shape=[jax.ShapeDtypeStruct((n, e), BF16), jax.ShapeDtypeStruct((n, LANES), F32)],
        compiler_params=pltpu.CompilerParams(dimension_semantics=("arbitrary",),
                                             vmem_limit_bytes=VMEM_LIMIT),
        name="norm_proj_gate",
    )(x2, gain, w, wf, bf)


def _split3(x):
    x1 = x.astype(BF16)
    r1 = x - x1.astype(F32)
    x2 = r1.astype(BF16)
    x3 = (r1 - x2.astype(F32)).astype(BF16)
    return x1, x2, x3


def _cumsum_kernel(lf_ref, o_ref):
    s, h = lf_ref.shape
    row = lax.broadcasted_iota(jnp.int32, (TK, TK), 0)
    col = lax.broadcasted_iota(jnp.int32, (TK, TK), 1)
    lower = jnp.where(col <= row, 1.0, 0.0).astype(BF16)
    carry = jnp.zeros((1, h), F32)
    for c in range(0, s, TK):
        x = lf_ref[c:c + TK, :]
        x1, x2, x3 = _split3(x)
        cs = _dot(lower, x1) + _dot(lower, x2) + _dot(lower, x3)
        o_ref[c:c + TK, :] = cs + carry
        carry = carry + jnp.sum(x, axis=0, keepdims=True)


def _cumsum_seq(lf, seq):
    n, h = lf.shape
    return pl.pallas_call(
        _cumsum_kernel,
        grid=(n // seq,),
        in_specs=[pl.BlockSpec((seq, h), lambda b: (b, 0))],
        out_specs=pl.BlockSpec((seq, h), lambda b: (b, 0)),
        out_shape=jax.ShapeDtypeStruct((n, h), F32),
        compiler_params=pltpu.CompilerParams(dimension_semantics=("arbitrary",)),
        name="gate_cumsum",
    )(lf)


def _stack_heads(q):
    lane = lax.broadcasted_iota(jnp.int32, q.shape, 1)
    zero = jnp.zeros_like(q)
    return jnp.concatenate([jnp.where(lane < HEAD_DIM, q, zero), jnp.where(lane >= HEAD_DIM, q, zero)], axis=0)


def _unstack_heads(acc):
    tq = acc.shape[0] // 2
    lane = lax.broadcasted_iota(jnp.int32, (tq, LANES), 1)
    return jnp.where(lane < HEAD_DIM, acc[:tq], acc[tq:])


def _diag_mask(strict):
    r = lax.broadcasted_iota(jnp.int32, (2 * TQ, TK), 0)
    c = lax.broadcasted_iota(jnp.int32, (2 * TQ, TK), 1)
    t = jnp.where(r >= TQ, r - TQ, r)
    return c < t if strict else c <= t


def _pair(ref, g, rows):
    return ref[rows, g * LANES:(g + 1) * LANES]


def _halves(x):
    return x[:, :LANES], x[:, LANES:]


def _rowsum(x0, x1):
    return jnp.broadcast_to(jnp.sum(x0 + x1, axis=1, keepdims=True), x0.shape)


def _neg_abs(x):
    bits = lax.bitcast_convert_type(x, jnp.uint32) | jnp.uint32(0x80000000)
    return lax.bitcast_convert_type(bits, F32)


def _sb_attn_kernel(q_ref, k_ref, v_ref, tri_ref, o_ref, q2_scr, carry_scr, acc_scr):
    qi = pl.program_id(2)
    for g in range(PAIRS):
        q2_scr[g] = _stack_heads(_pair(q_ref, g, slice(None)))
        carry_scr[g] = jnp.zeros((2 * TQ, LANES), F32)
        acc_scr[g] = jnp.zeros((2 * TQ, LANES), F32)

    def block(j, mask):
        rows = pl.ds(pl.multiple_of(j * TK, TK), TK)
        top = None
        for g in range(PAIRS):
            z = _dot_nt(q2_scr[g], _pair(k_ref, g, rows))
            log_beta =jnp.minimum(z, 0.0) - jnp.log2(1.0 + jnp.exp2(_neg_abs(z)))
            log_rem = log_beta - z
            if mask is not None:
                log_rem = jnp.where(mask, log_rem, 0.0)
            after = _dot(log_rem.astype(BF16), tri_ref[...])
            carry = carry_scr[g]
            t0, t1 = _halves(log_beta + after)
            w = jnp.exp2(jnp.concatenate([t0 + carry, t1 + carry], axis=1))
            if mask is not None:
                w = jnp.where(mask, w, 0.0)
            acc_scr[g] += _dot(w.astype(BF16), _pair(v_ref, g, rows))
            carry = carry + _rowsum(*_halves(log_rem))
            carry_scr[g] = carry
            top_g = jnp.max(carry, axis=0, keepdims=True)
            top = top_g if top is None else jnp.maximum(top, top_g)
        return jnp.max(top)

    top = block(qi, _diag_mask(strict=True))

    def cond(state):
        return jnp.logical_and(state[0] < qi, state[1] > UNDERFLOW_LOG2)

    def body(state):
        return state[0] + 1, block(qi - 1 - state[0], None)

    lax.while_loop(cond, body, (jnp.int32(0), top))
    for g in range(PAIRS):
        o_ref[:, g * LANES:(g + 1) * LANES] = _unstack_heads(acc_scr[g]).astype(o_ref.dtype)


def _fox_attn_kernel(q_ref, k_ref, v_ref, cum_ref, o_ref, q2_scr, m_scr, l_scr, acc_scr):
    qi = pl.program_id(2)
    for g in range(PAIRS):
        q2_scr[g] = _stack_heads(_pair(q_ref, g, slice(None)))
        m_scr[g] = jnp.full((2 * TQ, LANES), -jnp.inf, F32)
        l_scr[g] = jnp.zeros((2 * TQ, LANES), F32)
        acc_scr[g] = jnp.zeros((2 * TQ, LANES), F32)

    def blocks(js, mask):
        rows = [pl.ds(pl.multiple_of(j * TK, TK), TK) for j in js]
        for g in range(PAIRS):
            halves = []
            for j, r in zip(js, rows):
                cs = cum_ref[g, j]
                bias = jnp.concatenate([jnp.broadcast_to(cs[0:1], (TQ, TK)),
                                        jnp.broadcast_to(cs[1:2], (TQ, TK))], axis=0)
                s = _dot_nt(q2_scr[g], _pair(k_ref, g, r)) - bias
                if mask is not None:
                    s = jnp.where(mask, s, -jnp.inf)
                halves.extend(_halves(s))
            smax = functools.reduce(jnp.maximum, halves)
            m_old = m_scr[g]
            m_new = jnp.maximum(m_old, jnp.max(smax, axis=1, keepdims=True))
            alpha = jnp.exp2(m_old - m_new)
            ps = [jnp.exp2(h - m_new) for h in halves]
            psum = functools.reduce(jnp.add, ps)
            l_scr[g] = alpha * l_scr[g] + jnp.broadcast_to(jnp.sum(psum, axis=1, keepdims=True), psum.shape)
            acc = alpha * acc_scr[g]
            for i, r in enumerate(rows):
                p = jnp.concatenate(ps[2 * i:2 * i + 2], axis=1).astype(BF16)
                acc = acc + _dot(p, _pair(v_ref, g, r))
            acc_scr[g] = acc
            m_scr[g] = m_new

    blocks([qi], _diag_mask(strict=False))

    def body(i, _):
        j = qi - 1 - 2 * i
        blocks([j, j - 1], None)
        return 0

    lax.fori_loop(0, qi // 2, body, 0)

    @pl.when(qi % 2 == 1)
    def _():
        blocks([0], None)

    for g in range(PAIRS):
        o_ref[:, g * LANES:(g + 1) * LANES] = _unstack_heads(acc_scr[g] / l_scr[g]).astype(o_ref.dtype)


def _attention(kernel_fn, n_stats, proj, batch, seq, width, extra, extra_spec, name):
    gw = PAIRS * LANES
    n_groups = width // gw
    proj3 = proj.reshape(batch, seq, proj.shape[1])
    return pl.pallas_call(
        kernel_fn,
        grid=(batch, n_groups, seq // TQ),
        in_specs=[pl.BlockSpec((None, TQ, gw), lambda b, p, i: (b, i, p)),
                  pl.BlockSpec((None, seq, gw), lambda b, p, i: (b, 0, n_groups + p)),
                  pl.BlockSpec((None, seq, gw), lambda b, p, i: (b, 0, 2 * n_groups + p)),
                  extra_spec],
        out_specs=pl.BlockSpec((None, TQ, gw), lambda b, p, i: (b, i, p)),
        out_shape=jax.ShapeDtypeStruct((batch, seq, width), BF16),
        scratch_shapes=[pltpu.VMEM((PAIRS, 2 * TQ, LANES), BF16)]
        + [pltpu.VMEM((PAIRS, 2 * TQ, LANES), F32)] * n_stats,
        compiler_params=pltpu.CompilerParams(dimension_semantics=("arbitrary",) * 3,
                                             vmem_limit_bytes=VMEM_LIMIT),
        name=name,
    )(proj3, proj3, proj3, extra).reshape(batch * seq, width)


def _gated_out_kernel(o_ref, g_ref, x_ref, w_ref, fg_ref, out_ref, *, final_norm):
    og = (o_ref[...].astype(F32) * _silu(g_ref[...].astype(F32))).astype(BF16)
    y = x_ref[...] + _dot(og, w_ref[...])
    if final_norm:
        y = _rms(y, fg_ref[...])
    out_ref[...] = y


def _gated_out(o, proj, g_block, x2, w, final_gain, final_norm, tm=512):
    n, d = x2.shape
    e = o.shape[1]
    return pl.pallas_call(
        functools.partial(_gated_out_kernel, final_norm=final_norm),
        grid=(n // tm,),
        in_specs=[pl.BlockSpec((tm, e), lambda i: (i, 0)),
                  pl.BlockSpec((tm, e), lambda i: (i, g_block)),
                  pl.BlockSpec((tm, d), lambda i: (i, 0)),
                  _resident((e, d)), _resident((1, d))],
        out_specs=pl.BlockSpec((tm, d), lambda i: (i, 0)),
        out_shape=jax.ShapeDtypeStruct((n, d), F32),
        compiler_params=pltpu.CompilerParams(dimension_semantics=("arbitrary",),
                                             vmem_limit_bytes=VMEM_LIMIT),
        name="gated_out",
    )(o, proj, x2, w, final_gain)


def _gmlp_kernel(x_ref, gain_ref, win_ref, lng_ref, lnb_ref, ws_ref, bias_ref, wout_ref, out_ref,
                 v_scr, vn_scr, s_scr, og_scr, *, n_chunk):
    tm = x_ref.shape[0]
    width = wout_ref.shape[0]
    gw = width // GM_GROUPS
    hn = _rms(x_ref[...], gain_ref[...]).astype(BF16)

    for c in range(0, width, n_chunk):
        v_scr[:, c:c + n_chunk] = _gelu_tanh(_dot(hn, win_ref[:, width + c:width + c + n_chunk]))
    v = v_scr[...]
    mu = jnp.mean(v, axis=-1, keepdims=True)
    vc = v - mu
    var = jnp.mean(vc * vc, axis=-1, keepdims=True)
    vn_scr[...] = (vc * lax.rsqrt(var + EPS) * lng_ref[...] + lnb_ref[...]).astype(BF16)

    row = lax.broadcasted_iota(jnp.int32, (GM_CHUNK, GM_CHUNK), 0)
    col = lax.broadcasted_iota(jnp.int32, (GM_CHUNK, GM_CHUNK), 1)
    for g in range(GM_GROUPS):
        wsm = jnp.where(row >= col, ws_ref[g], 0.0).astype(BF16)
        for n in range(0, tm, GM_CHUNK):
            s_scr[n:n + GM_CHUNK, g * gw:(g + 1) * gw] = (
                _dot(wsm, vn_scr[n:n + GM_CHUNK, g * gw:(g + 1) * gw]) + bias_ref[:, g * gw:(g + 1) * gw])

    for c in range(0, width, n_chunk):
        u = _gelu_tanh(_dot(hn, win_ref[:, c:c + n_chunk]))
        gate = _dot(hn, win_ref[:, 2 * width + c:2 * width + c + n_chunk])
        og_scr[:, c:c + n_chunk] = ((u * s_scr[:, c:c + n_chunk]) * _silu(gate)).astype(BF16)

    out_ref[...] = x_ref[...] + _dot(og_scr[...], wout_ref[...])


def _gmlp_layer(x2, gain, win, lng, lnb, ws, bias_full, wout, tm=256, n_chunk=512):
    n, d = x2.shape
    width = wout.shape[0]
    return pl.pallas_call(
        functools.partial(_gmlp_kernel, n_chunk=n_chunk),
        grid=(n // tm,),
        in_specs=[pl.BlockSpec((tm, d), lambda i: (i, 0)), _resident((1, d)), _resident(win.shape),
                  _resident((1, width)), _resident((1, width)), _resident(ws.shape),
                  _resident(bias_full.shape), _resident(wout.shape)],
        out_specs=pl.BlockSpec((tm, d), lambda i: (i, 0)),
        out_shape=jax.ShapeDtypeStruct((n, d), F32),
        scratch_shapes=[pltpu.VMEM((tm, width), F32), pltpu.VMEM((tm, width), BF16),
                        pltpu.VMEM((tm, width), F32), pltpu.VMEM((tm, width), BF16)],
        compiler_params=pltpu.CompilerParams(dimension_semantics=("arbitrary",),
                                             vmem_limit_bytes=VMEM_LIMIT),
        name="gmlp_layer",
    )(x2, gain, win, lng, lnb, ws, bias_full, wout)


def _conv_kernel(x_ref, gain_ref, win_ref, cw_ref, cb_ref, lng_ref, lnb_ref, wout_ref, out_ref,
                 y_scr, c_scr, g_scr, og_scr, *, n_chunk, row_chunk):
    tm = x_ref.shape[0]
    width = wout_ref.shape[0]
    n_lane_chunks = width // LANES
    t_idx = pl.program_id(1)
    hn = _rms(x_ref[...], gain_ref[...]).astype(BF16)

    @pl.when(t_idx == 0)
    def _():
        y_scr[:, 0:HALO, :] = jnp.zeros((n_lane_chunks, HALO, LANES), F32)

    for c in range(0, width, n_chunk):
        a = _dot(hn, win_ref[:, c:c + n_chunk])
        b = _dot(hn, win_ref[:, width + c:width + c + n_chunk])
        y = a * _sigmoid(b)
        g_scr[:, c:c + n_chunk] = _dot(hn, win_ref[:, 2 * width + c:2 * width + c + n_chunk])
        for i in range(n_chunk // LANES):
            ci = c // LANES + i
            y_scr[ci, HALO:HALO + tm, :] = y[:, i * LANES:(i + 1) * LANES]
            for r in range(0, tm, row_chunk):
                acc = jnp.zeros((row_chunk, LANES), F32)
                for k in range(CV_KERNEL):
                    acc = acc + cw_ref[ci, k:k + 1, :] * y_scr[ci, HALO - (CV_KERNEL - 1) + k + r:
                                                               HALO - (CV_KERNEL - 1) + k + r + row_chunk, :]
                c_scr[ci, r:r + row_chunk, :] = acc + cb_ref[ci]
            y_scr[ci, 0:HALO, :] = y_scr[ci, tm:tm + HALO, :]

    tot = c_scr[0]
    for ci in range(1, n_lane_chunks):
        tot = tot + c_scr[ci]
    mu = jnp.sum(tot, axis=-1, keepdims=True) * (1.0 / width)
    d0 = c_scr[0] - mu
    sq = d0 * d0
    for ci in range(1, n_lane_chunks):
        dc = c_scr[ci] - mu
        sq = sq + dc * dc
    rstd = lax.rsqrt(jnp.sum(sq, axis=-1, keepdims=True) * (1.0 / width) + EPS)

    for ci in range(n_lane_chunks):
        lo = ci * LANES
        yn = (c_scr[ci] - mu) * rstd * lng_ref[:, lo:lo + LANES] + lnb_ref[:, lo:lo + LANES]
        og_scr[:, lo:lo + LANES] = (_silu(yn) * _silu(g_scr[:, lo:lo + LANES])).astype(BF16)

    out_ref[...] = x_ref[...] + _dot(og_scr[...], wout_ref[...])


def _conv_layer(x3, gain, win, cw, cb, lng, lnb, wout, tm=256, n_chunk=256, row_chunk=128):
    batch, seq, d = x3.shape
    width = wout.shape[0]
    n_lane_chunks = width // LANES
    return pl.pallas_call(
        functools.partial(_conv_kernel, n_chunk=n_chunk, row_chunk=row_chunk),
        grid=(batch, seq // tm),
        in_specs=[pl.BlockSpec((None, tm, d), lambda b, t: (b, t, 0)), _resident((1, d)), _resident(win.shape),
                  _resident(cw.shape), _resident(cb.shape), _resident((1, width)), _resident((1, width)),
                  _resident(wout.shape)],
        out_specs=pl.BlockSpec((None, tm, d), lambda b, t: (b, t, 0)),
        out_shape=jax.ShapeDtypeStruct((batch, seq, d), F32),
        scratch_shapes=[pltpu.VMEM((n_lane_chunks, tm + HALO, LANES), F32),
                        pltpu.VMEM((n_lane_chunks, tm, LANES), F32),
                        pltpu.VMEM((tm, width), F32),
                        pltpu.VMEM((tm, width), BF16)],
        compiler_params=pltpu.CompilerParams(dimension_semantics=("arbitrary", "arbitrary"),
                                             vmem_limit_bytes=VMEM_LIMIT),
        name="conv_layer",
    )(x3, gain, win, cw, cb, lng, lnb, wout)


def _scale_q_columns(w_in, width):
    col = lax.broadcasted_iota(jnp.int32, (1, w_in.shape[1]), 1)
    return w_in * jnp.where(col < width, Q_SCALE, 1.0)


def _layer_stick_breaking(x2, batch, seq, gain, w_in, w_out, final_gain, final_norm):
    width = w_out.shape[0]
    proj = _norm_proj(x2, gain[None, :], _scale_q_columns(w_in, width).astype(BF16))
    row = lax.broadcasted_iota(jnp.int32, (TK, TK), 0)
    col = lax.broadcasted_iota(jnp.int32, (TK, TK), 1)
    tri = (row > col).astype(BF16)
    o = _attention(_sb_attn_kernel, 2, proj, batch, seq, width, tri, _resident((TK, TK)), "sb_attention")
    return _gated_out(o, proj, 3, x2, w_out.astype(BF16), final_gain, final_norm)


def _layer_forgetting(x2, batch, seq, gain, w_in, b_f, w_out, final_gain, final_norm):
    width = w_out.shape[0]
    n_heads = b_f.shape[0]
    d = w_in.shape[0]
    w_main = _scale_q_columns(w_in[:, :4 * width], width).astype(BF16)
    w_f = jnp.zeros((d, LANES), BF16).at[:, :n_heads].set(w_in[:, 4 * width:].astype(BF16))
    b_pad = jnp.zeros((1, LANES), F32).at[0, :n_heads].set(b_f)
    proj, lf = _norm_proj_gate(x2, gain[None, :], w_main, w_f, b_pad)
    cum = _cumsum_seq(lf, seq)
    n_pairs = n_heads // 2
    cum = cum[:, :n_heads].reshape(batch, seq // TK, TK, n_pairs, 2).transpose(0, 3, 1, 4, 2)
    cum_spec = pl.BlockSpec((None, PAIRS, seq // TK, 2, TK), lambda b, p, i: (b, p, 0, 0, 0))
    o = _attention(_fox_attn_kernel, 3, proj, batch, seq, width, cum, cum_spec, "fox_attention")
    return _gated_out(o, proj, 3, x2, w_out.astype(BF16), final_gain, final_norm)


def _layer_gmlp(x2, gain, w_in, ln_g, ln_b, w_s, b_s, w_out):
    width = w_out.shape[0]
    bias_full = jnp.repeat(b_s.T, width // GM_GROUPS, axis=1)
    return _gmlp_layer(x2, gain[None, :], w_in.astype(BF16), ln_g[None, :], ln_b[None, :], w_s, bias_full,
                       w_out.astype(BF16))


def _layer_conv(x2, batch, seq, gain, w_in, conv_w, conv_b, ln_g, ln_b, w_out):
    width = w_out.shape[0]
    n_lane_chunks = width // LANES
    d = x2.shape[1]
    cw = jnp.zeros((HALO, width), F32).at[:CV_KERNEL].set(conv_w)
    cw = cw.reshape(HALO, n_lane_chunks, LANES).transpose(1, 0, 2)
    cb = conv_b.reshape(n_lane_chunks, 1, LANES)
    y = _conv_layer(x2.reshape(batch, seq, d), gain[None, :], w_in.astype(BF16), cw, cb, ln_g[None, :],
                    ln_b[None, :], w_out.astype(BF16))
    return y.reshape(batch * seq, d)


def kernel(x, a_norm, a_w_in, a_w_out, b_norm, b_w_in, b_v_ln_g, b_v_ln_b, b_w_s, b_b_s, b_w_out, c_norm, c_w_in, c_conv_w, c_conv_b, c_ln_g, c_ln_b, c_w_out, d_norm, d_w_in, d_b_f, d_w_out, final_norm):
    batch, seq, d = x.shape
    counts = (a_norm.shape[0], b_norm.shape[0], c_norm.shape[0], d_norm.shape[0])
    depth = sum(counts)
    fg = final_norm[None, :]
    x2 = x.reshape(batch * seq, d)
    for i in range(depth):
        kind, j = i % 4, i // 4
        last = i == depth - 1
        if kind == 0:
            x2 = _layer_stick_breaking(x2, batch, seq, a_norm[j], a_w_in[j], a_w_out[j], fg, last)
        elif kind == 1:
            x2 = _layer_gmlp(x2, b_norm[j], b_w_in[j], b_v_ln_g[j], b_v_ln_b[j], b_w_s[j], b_b_s[j], b_w_out[j])
        elif kind == 2:
            x2 = _layer_conv(x2, batch, seq, c_norm[j], c_w_in[j], c_conv_w[j], c_conv_b[j], c_ln_g[j],
                             c_ln_b[j], c_w_out[j])
        else:
            x2 = _layer_forgetting(x2, batch, seq, d_norm[j], d_w_in[j], d_b_f[j], d_w_out[j], fg, last)
    return x2.reshape(batch, seq, d)
```

```python
import functools
import math

import jax
import jax.numpy as jnp
from jax import lax
from jax.experimental import pallas as pl
from jax.experimental.pallas import tpu as pltpu

EPS = 1e-6
LANES = 128
HEAD_DIM = 64
GM_CHUNK = 128
GM_GROUPS = 16
CV_KERNEL = 31
HALO = 32
TQ = 256
TK = 256
PAIRS = 8
VMEM_LIMIT = 56 * 1024 * 1024
LOG2E = math.log2(math.e)
UNDERFLOW_LOG2 = -150.0
Q_SCALE = LOG2E / math.sqrt(HEAD_DIM)

F32 = jnp.float32
BF16 = jnp.bfloat16


def _resident(shape):
    nd = len(shape)
    return pl.BlockSpec(shape, lambda *_: (0,) * nd, pipeline_mode=pl.Buffered(1))


def _rms(x, gain):
    return x * lax.rsqrt(jnp.mean(x * x, axis=-1, keepdims=True) + EPS) * gain


def _silu_of_half(h):
    return h + h * jnp.tanh(h)


def _silu(x):
    return _silu_of_half(0.5 * x)


def _gelu_tanh(x):
    c = math.sqrt(2.0 / math.pi)
    return 0.5 * x * (1.0 + jnp.tanh(c * (x + 0.044715 * (x * x * x))))


def _log_sigmoid(x):
    return jnp.minimum(x, 0.0) - jnp.log(1.0 + jnp.exp(-jnp.abs(x)))


def _dot(a, b):
    return jnp.dot(a, b, preferred_element_type=F32)


def _dot_nt(a, b):
    return lax.dot_general(a, b, (((1,), (1,)), ((), ())), preferred_element_type=F32)


def _norm_proj_kernel(x_ref, gain_ref, w_ref, o_ref, *, n_chunk):
    hn = _rms(x_ref[...], gain_ref[...]).astype(BF16)
    for c in range(0, o_ref.shape[1], n_chunk):
        o_ref[:, c:c + n_chunk] = _dot(hn, w_ref[:, c:c + n_chunk]).astype(o_ref.dtype)


def _norm_proj_gate_kernel(x_ref, gain_ref, w_ref, wf_ref, bf_ref, o_ref, lf_ref, *, n_chunk):
    hn = _rms(x_ref[...], gain_ref[...]).astype(BF16)
    for c in range(0, o_ref.shape[1], n_chunk):
        o_ref[:, c:c + n_chunk] = _dot(hn, w_ref[:, c:c + n_chunk]).astype(o_ref.dtype)
    lf_ref[...] = _log_sigmoid(_dot(hn, wf_ref[...]) + bf_ref[...]) * LOG2E


def _norm_proj(x2, gain, w, tm=512, n_chunk=1024):
    n, d = x2.shape
    e = w.shape[1]
    return pl.pallas_call(
        functools.partial(_norm_proj_kernel, n_chunk=n_chunk),
        grid=(n // tm,),
        in_specs=[pl.BlockSpec((tm, d), lambda i: (i, 0)), _resident((1, d)), _resident((d, e))],
        out_specs=pl.BlockSpec((tm, e), lambda i: (i, 0)),
        out_shape=jax.ShapeDtypeStruct((n, e), BF16),
        compiler_params=pltpu.CompilerParams(dimension_semantics=("arbitrary",),
                                             vmem_limit_bytes=VMEM_LIMIT),
        name="norm_proj",
    )(x2, gain, w)


def _norm_proj_gate(x2, gain, w, wf, bf, tm=512, n_chunk=1024):
    n, d = x2.shape
    e = w.shape[1]
    return pl.pallas_call(
        functools.partial(_norm_proj_gate_kernel, n_chunk=n_chunk),
        grid=(n // tm,),
        in_specs=[pl.BlockSpec((tm, d), lambda i: (i, 0)), _resident((1, d)), _resident((d, e)),
                  _resident((d, LANES)), _resident((1, LANES))],
        out_specs=[pl.BlockSpec((tm, e), lambda i: (i, 0)),
                   pl.BlockSpec((tm, LANES), lambda i: (i, 0))],
        out_shape=[jax.ShapeDtypeStruct((n, e), BF16), jax.ShapeDtypeStruct((n, LANES), F32)],
        compiler_params=pltpu.CompilerParams(dimension_semantics=("arbitrary",),
                                             vmem_limit_bytes=VMEM_LIMIT),
        name="norm_proj_gate",
    )(x2, gain, w, wf, bf)


def _split3(x):
    x1 = x.astype(BF16)
    r1 = x - x1.astype(F32)
    x2 = r1.astype(BF16)
    x3 = (r1 - x2.astype(F32)).astype(BF16)
    return x1, x2, x3


def _cumsum_kernel(lf_ref, o_ref):
    s, h = lf_ref.shape
    row = lax.broadcasted_iota(jnp.int32, (TK, TK), 0)
    col = lax.broadcasted_iota(jnp.int32, (TK, TK), 1)
    lower = jnp.where(col <= row, 1.0, 0.0).astype(BF16)
    carry = jnp.zeros((1, h), F32)
    for c in range(0, s, TK):
        x = lf_ref[c:c + TK, :]
        x1, x2, x3 = _split3(x)
        cs = _dot(lower, x1) + _dot(lower, x2) + _dot(lower, x3)
        o_ref[c:c + TK, :] = cs + carry
        carry = carry + jnp.sum(x, axis=0, keepdims=True)


def _cumsum_seq(lf, seq):
    n, h = lf.shape
    return pl.pallas_call(
        _cumsum_kernel,
        grid=(n // seq,),
        in_specs=[pl.BlockSpec((seq, h), lambda b: (b, 0))],
        out_specs=pl.BlockSpec((seq, h), lambda b: (b, 0)),
        out_shape=jax.ShapeDtypeStruct((n, h), F32),
        compiler_params=pltpu.CompilerParams(dimension_semantics=("arbitrary",)),
        name="gate_cumsum",
    )(lf)


def _stack_heads(q):
    lane = lax.broadcasted_iota(jnp.int32, q.shape, 1)
    zero = jnp.zeros_like(q)
    return jnp.concatenate([jnp.where(lane < HEAD_DIM, q, zero), jnp.where(lane >= HEAD_DIM, q, zero)], axis=0)


def _unstack_heads(acc):
    tq = acc.shape[0] // 2
    lane = lax.broadcasted_iota(jnp.int32, (tq, LANES), 1)
    return jnp.where(lane < HEAD_DIM, acc[:tq], acc[tq:])


def _diag_mask(strict):
    r = lax.broadcasted_iota(jnp.int32, (2 * TQ, TK), 0)
    c = lax.broadcasted_iota(jnp.int32, (2 * TQ, TK), 1)
    t = jnp.where(r >= TQ, r - TQ, r)
    return c < t if strict else c <= t


def _pair(ref, g, rows):
    return ref[rows, g * LANES:(g + 1) * LANES]


def _halves(x):
    return x[:, :LANES], x[:, LANES:]


def _rowsum(x0, x1):
    return jnp.broadcast_to(jnp.sum(x0 + x1, axis=1, keepdims=True), x0.shape)


def _neg_abs(x):
    bits = lax.bitcast_convert_type(x, jnp.uint32) | jnp.uint32(0x80000000)
    return lax.bitcast_convert_type(bits, F32)


def _sb_attn_kernel(q_ref, k_ref, v_ref, tri_ref, g_ref, x_ref, wout_ref, fg_ref, out_ref,
                    q2_scr, og_scr, carry_scr, acc_scr, *, final_norm):
    qi = pl.program_id(1)
    for g in range(PAIRS):
        q2_scr[g] = _stack_heads(_pair(q_ref, g, slice(None)))
        carry_scr[g] = jnp.zeros((2 * TQ, LANES), F32)
        acc_scr[g] = jnp.zeros((2 * TQ, LANES), F32)

    def block(j, mask):
        rows = pl.ds(pl.multiple_of(j * TK, TK), TK)
        top = None
        for g in range(PAIRS):
            z = _dot_nt(q2_scr[g], _pair(k_ref, g, rows))
            log_beta =jnp.minimum(z, 0.0) - jnp.log2(1.0 + jnp.exp2(_neg_abs(z)))
            log_rem = log_beta - z
            if mask is not None:
                log_rem = jnp.where(mask, log_rem, 0.0)
            after = _dot(log_rem.astype(BF16), tri_ref[...])
            carry = carry_scr[g]
            t0, t1 = _halves(log_beta + after)
            w = jnp.exp2(jnp.concatenate([t0 + carry, t1 + carry], axis=1))
            if mask is not None:
                w = jnp.where(mask, w, 0.0)
            acc_scr[g] += _dot(w.astype(BF16), _pair(v_ref, g, rows))
            carry = carry + _rowsum(*_halves(log_rem))
            carry_scr[g] = carry
            top_g = jnp.max(carry, axis=0, keepdims=True)
            top = top_g if top is None else jnp.maximum(top, top_g)
        return jnp.max(top)

    top = block(qi, _diag_mask(strict=True))

    def cond(state):
        return jnp.logical_and(state[0] < qi, state[1] > UNDERFLOW_LOG2)

    def body(state):
        return state[0] + 1, block(qi - 1 - state[0], None)

    lax.while_loop(cond, body, (jnp.int32(0), top))
    _gated_out(lambda g: _unstack_heads(acc_scr[g]), g_ref, x_ref, wout_ref, fg_ref, out_ref, og_scr, final_norm)


def _gated_out(attn_out, g_ref, x_ref, wout_ref, fg_ref, out_ref, og_scr, final_norm):
    for g in range(PAIRS):
        cols = slice(g * LANES, (g + 1) * LANES)
        og_scr[:, cols] = (attn_out(g) * _silu(g_ref[:, cols].astype(F32))).astype(BF16)
    y = x_ref[...] + _dot(og_scr[...], wout_ref[...])
    if final_norm:
        y = _rms(y, fg_ref[...])
    out_ref[...] = y


def _fox_attn_kernel(q_ref, k_ref, v_ref, cum_ref, g_ref, x_ref, wout_ref, fg_ref, out_ref,
                     q2_scr, og_scr, m_scr, l_scr, acc_scr, *, final_norm):
    qi = pl.program_id(1)
    for g in range(PAIRS):
        q2_scr[g] = _stack_heads(_pair(q_ref, g, slice(None)))
        m_scr[g] = jnp.full((2 * TQ, LANES), -jnp.inf, F32)
        l_scr[g] = jnp.zeros((2 * TQ, LANES), F32)
        acc_scr[g] = jnp.zeros((2 * TQ, LANES), F32)

    def blocks(js, mask):
        rows = [pl.ds(pl.multiple_of(j * TK, TK), TK) for j in js]
        for g in range(PAIRS):
            halves = []
            for j, r in zip(js, rows):
                cs = cum_ref[g, j]
                bias = jnp.concatenate([jnp.broadcast_to(cs[0:1], (TQ, TK)),
                                        jnp.broadcast_to(cs[1:2], (TQ, TK))], axis=0)
                s = _dot_nt(q2_scr[g], _pair(k_ref, g, r)) - bias
                if mask is not None:
                    s = jnp.where(mask, s, -jnp.inf)
                halves.extend(_halves(s))
            smax = functools.reduce(jnp.maximum, halves)
            m_old = m_scr[g]
            m_new = jnp.maximum(m_old, jnp.max(smax, axis=1, keepdims=True))
            alpha = jnp.exp2(m_old - m_new)
            ps = [jnp.exp2(h - m_new) for h in halves]
            psum = functools.reduce(jnp.add, ps)
            l_scr[g] = alpha * l_scr[g] + jnp.broadcast_to(jnp.sum(psum, axis=1, keepdims=True), psum.shape)
            acc = alpha * acc_scr[g]
            for i, r in enumerate(rows):
                p = jnp.concatenate(ps[2 * i:2 * i + 2], axis=1).astype(BF16)
                acc = acc + _dot(p, _pair(v_ref, g, r))
            acc_scr[g] = acc
            m_scr[g] = m_new

    blocks([qi], _diag_mask(strict=False))

    def body(i, _):
        j = qi - 1 - 2 * i
        blocks([j, j - 1], None)
        return 0

    lax.fori_loop(0, qi // 2, body, 0)

    @pl.when(qi % 2 == 1)
    def _():
        blocks([0], None)

    _gated_out(lambda g: _unstack_heads(acc_scr[g] / l_scr[g]), g_ref, x_ref, wout_ref, fg_ref, out_ref, og_scr,
               final_norm)


def _attention_layer(kernel_fn, n_stats, proj, x2, w_out, final_gain, final_norm, batch, seq, extra, extra_spec,
                     name):
    width, d = w_out.shape
    assert width == PAIRS * LANES
    proj3 = proj.reshape(batch, seq, proj.shape[1])
    col = lambda c: pl.BlockSpec((None, TQ, width), lambda b, i: (b, i, c))
    seq_col = lambda c: pl.BlockSpec((None, seq, width), lambda b, i: (b, 0, c))
    return pl.pallas_call(
        functools.partial(kernel_fn, final_norm=final_norm),
        grid=(batch, seq // TQ),
        in_specs=[col(0), seq_col(1), seq_col(2), extra_spec, col(3),
                  pl.BlockSpec((None, TQ, d), lambda b, i: (b, i, 0)), _resident((width, d)), _resident((1, d))],
        out_specs=pl.BlockSpec((None, TQ, d), lambda b, i: (b, i, 0)),
        out_shape=jax.ShapeDtypeStruct((batch, seq, d), F32),
        scratch_shapes=[pltpu.VMEM((PAIRS, 2 * TQ, LANES), BF16), pltpu.VMEM((TQ, width), BF16)]
        + [pltpu.VMEM((PAIRS, 2 * TQ, LANES), F32)] * n_stats,
        compiler_params=pltpu.CompilerParams(dimension_semantics=("arbitrary", "arbitrary"),
                                             vmem_limit_bytes=VMEM_LIMIT),
        name=name,
    )(proj3, proj3, proj3, extra, proj3, x2.reshape(batch, seq, d), w_out, final_gain).reshape(batch * seq, d)


def _gmlp_kernel(x_ref, gain_ref, win_ref, lng_ref, lnb_ref, ws_ref, bias_ref, wout_ref, out_ref,
                 v_scr, u_scr, g_scr, vn_scr, s_scr, *, n_chunk):
    tm = x_ref.shape[0]
    width = wout_ref.shape[0]
    gw = width // GM_GROUPS
    hn = _rms(x_ref[...], gain_ref[...]).astype(BF16)

    for c in range(0, width, n_chunk):
        v_scr[:, c:c + n_chunk] = _gelu_tanh(_dot(hn, win_ref[:, width + c:width + c + n_chunk]))
    for c in range(0, width, n_chunk):
        u_scr[:, c:c + n_chunk] = _dot(hn, win_ref[:, c:c + n_chunk])
        g_scr[:, c:c + n_chunk] = _dot(hn, win_ref[:, 2 * width + c:2 * width + c + n_chunk])
    v = v_scr[...]
    mu = jnp.mean(v, axis=-1, keepdims=True)
    vc = v - mu
    var = jnp.mean(vc * vc, axis=-1, keepdims=True)
    vn_scr[...] = (vc * lax.rsqrt(var + EPS) * lng_ref[...] + lnb_ref[...]).astype(BF16)

    row = lax.broadcasted_iota(jnp.int32, (GM_CHUNK, GM_CHUNK), 0)
    col = lax.broadcasted_iota(jnp.int32, (GM_CHUNK, GM_CHUNK), 1)
    for g in range(GM_GROUPS):
        wsm = jnp.where(row >= col, ws_ref[g], 0.0).astype(BF16)
        for n in range(0, tm, GM_CHUNK):
            s_scr[n:n + GM_CHUNK, g * gw:(g + 1) * gw] = (
                _dot(wsm, vn_scr[n:n + GM_CHUNK, g * gw:(g + 1) * gw]) + bias_ref[:, g * gw:(g + 1) * gw])

    y = x_ref[...]
    for c in range(0, width, n_chunk):
        u = _gelu_tanh(u_scr[:, c:c + n_chunk])
        og = ((u * s_scr[:, c:c + n_chunk]) * _silu(g_scr[:, c:c + n_chunk])).astype(BF16)
        y = y + _dot(og, wout_ref[c:c + n_chunk, :])
    out_ref[...] = y


def _gmlp_layer(x2, gain, win, lng, lnb, ws, bias_full, wout, tm=512, n_chunk=512):
    n, d = x2.shape
    width = wout.shape[0]
    return pl.pallas_call(
        functools.partial(_gmlp_kernel, n_chunk=n_chunk),
        grid=(n // tm,),
        in_specs=[pl.BlockSpec((tm, d), lambda i: (i, 0)), _resident((1, d)), _resident(win.shape),
                  _resident((1, width)), _resident((1, width)), _resident(ws.shape),
                  _resident(bias_full.shape), _resident(wout.shape)],
        out_specs=pl.BlockSpec((tm, d), lambda i: (i, 0)),
        out_shape=jax.ShapeDtypeStruct((n, d), F32),
        scratch_shapes=[pltpu.VMEM((tm, width), F32), pltpu.VMEM((tm, width), F32), pltpu.VMEM((tm, width), F32),
                        pltpu.VMEM((tm, width), BF16), pltpu.VMEM((tm, width), F32)],
        compiler_params=pltpu.CompilerParams(dimension_semantics=("arbitrary",),
                                             vmem_limit_bytes=VMEM_LIMIT),
        name="gmlp_layer",
    )(x2, gain, win, lng, lnb, ws, bias_full, wout)


def _conv_kernel(x_ref, gain_ref, win_ref, cw_ref, cb_ref, lng_ref, lnb_ref, wout_ref, out_ref,
                 y_scr, c_scr, g_scr, og_scr, *, n_chunk, row_chunk):
    tm = x_ref.shape[0]
    width = wout_ref.shape[0]
    n_lane_chunks = width // LANES
    t_idx = pl.program_id(1)
    hn = _rms(x_ref[...], gain_ref[...]).astype(BF16)

    @pl.when(t_idx == 0)
    def _():
        y_scr[:, 0:HALO, :] = jnp.zeros((n_lane_chunks, HALO, LANES), F32)

    for c in range(0, width, n_chunk):
        a = _dot(hn, win_ref[:, c:c + n_chunk])
        b = _dot(hn, win_ref[:, width + c:width + c + n_chunk])
        y = a + a * jnp.tanh(b)
        g_scr[:, c:c + n_chunk] = _dot(hn, win_ref[:, 2 * width + c:2 * width + c + n_chunk])
        for i in range(n_chunk // LANES):
            ci = c // LANES + i
            y_scr[ci, HALO:HALO + tm, :] = y[:, i * LANES:(i + 1) * LANES]
            for r in range(0, tm, row_chunk):
                acc = jnp.zeros((row_chunk, LANES), F32)
                for k in range(CV_KERNEL):
                    acc = acc + cw_ref[ci, k:k + 1, :] * y_scr[ci, HALO - (CV_KERNEL - 1) + k + r:
                                                               HALO - (CV_KERNEL - 1) + k + r + row_chunk, :]
                c_scr[ci, r:r + row_chunk, :] = acc + cb_ref[ci]
            y_scr[ci, 0:HALO, :] = y_scr[ci, tm:tm + HALO, :]

    tot = c_scr[0]
    for ci in range(1, n_lane_chunks):
        tot = tot + c_scr[ci]
    mu = jnp.sum(tot, axis=-1, keepdims=True) * (1.0 / width)
    d0 = c_scr[0] - mu
    sq = d0 * d0
    for ci in range(1, n_lane_chunks):
        dc = c_scr[ci] - mu
        sq = sq + dc * dc
    rstd = lax.rsqrt(jnp.sum(sq, axis=-1, keepdims=True) * (1.0 / width) + EPS)

    for ci in range(n_lane_chunks):
        lo = ci * LANES
        yh = (c_scr[ci] - mu) * rstd * lng_ref[:, lo:lo + LANES] + lnb_ref[:, lo:lo + LANES]
        og_scr[:, lo:lo + LANES] = (_silu_of_half(yh) * _silu_of_half(g_scr[:, lo:lo + LANES])).astype(BF16)

    out_ref[...] = x_ref[...] + _dot(og_scr[...], wout_ref[...])


def _conv_layer(x3, gain, win, cw, cb, lng, lnb, wout, tm=256, n_chunk=256, row_chunk=128):
    batch, seq, d = x3.shape
    width = wout.shape[0]
    n_lane_chunks = width // LANES
    return pl.pallas_call(
        functools.partial(_conv_kernel, n_chunk=n_chunk, row_chunk=row_chunk),
        grid=(batch, seq // tm),
        in_specs=[pl.BlockSpec((None, tm, d), lambda b, t: (b, t, 0)), _resident((1, d)), _resident(win.shape),
                  _resident(cw.shape), _resident(cb.shape), _resident((1, width)), _resident((1, width)),
                  _resident(wout.shape)],
        out_specs=pl.BlockSpec((None, tm, d), lambda b, t: (b, t, 0)),
        out_shape=jax.ShapeDtypeStruct((batch, seq, d), F32),
        scratch_shapes=[pltpu.VMEM((n_lane_chunks, tm + HALO, LANES), F32),
                        pltpu.VMEM((n_lane_chunks, tm, LANES), F32),
                        pltpu.VMEM((tm, width), F32),
                        pltpu.VMEM((tm, width), BF16)],
        compiler_params=pltpu.CompilerParams(dimension_semantics=("arbitrary", "arbitrary"),
                                             vmem_limit_bytes=VMEM_LIMIT),
        name="conv_layer",
    )(x3, gain, win, cw, cb, lng, lnb, wout)


def _scale_q_columns(w_in, width):
    col = lax.broadcasted_iota(jnp.int32, (1, w_in.shape[1]), 1)
    return w_in * jnp.where(col < width, Q_SCALE, 1.0)


def _layer_stick_breaking(x2, batch, seq, gain, w_in, w_out, final_gain, final_norm):
    width = w_out.shape[0]
    proj = _norm_proj(x2, gain[None, :], _scale_q_columns(w_in, width).astype(BF16))
    row = lax.broadcasted_iota(jnp.int32, (TK, TK), 0)
    col = lax.broadcasted_iota(jnp.int32, (TK, TK), 1)
    tri = (row > col).astype(BF16)
    return _attention_layer(_sb_attn_kernel, 2, proj, x2, w_out.astype(BF16), final_gain, final_norm, batch, seq,
                            tri, _resident((TK, TK)), "sb_attention")


def _layer_forgetting(x2, batch, seq, gain, w_in, b_f, w_out, final_gain, final_norm):
    width = w_out.shape[0]
    n_heads = b_f.shape[0]
    d = w_in.shape[0]
    w_main = _scale_q_columns(w_in[:, :4 * width], width).astype(BF16)
    w_f = jnp.zeros((d, LANES), BF16).at[:, :n_heads].set(w_in[:, 4 * width:].astype(BF16))
    b_pad = jnp.zeros((1, LANES), F32).at[0, :n_heads].set(b_f)
    proj, lf = _norm_proj_gate(x2, gain[None, :], w_main, w_f, b_pad)
    cum = _cumsum_seq(lf, seq)
    n_pairs = n_heads // 2
    cum = cum[:, :n_heads].reshape(batch, seq // TK, TK, n_pairs, 2).transpose(0, 3, 1, 4, 2)
    cum_spec = pl.BlockSpec((None, PAIRS, seq // TK, 2, TK), lambda b, i: (b, 0, 0, 0, 0))
    return _attention_layer(_fox_attn_kernel, 3, proj, x2, w_out.astype(BF16), final_gain, final_norm, batch, seq,
                            cum, cum_spec, "fox_attention")


def _layer_gmlp(x2, gain, w_in, ln_g, ln_b, w_s, b_s, w_out):
    width = w_out.shape[0]
    bias_full = jnp.repeat(b_s.T, width // GM_GROUPS, axis=1)
    return _gmlp_layer(x2, gain[None, :], w_in.astype(BF16), ln_g[None, :], ln_b[None, :], w_s, bias_full,
                       w_out.astype(BF16))


def _layer_conv(x2, batch, seq, gain, w_in, conv_w, conv_b, ln_g, ln_b, w_out):
    width = w_out.shape[0]
    n_lane_chunks = width // LANES
    d = x2.shape[1]
    cw = jnp.zeros((HALO, width), F32).at[:CV_KERNEL].set(conv_w)
    cw = cw.reshape(HALO, n_lane_chunks, LANES).transpose(1, 0, 2)
    cb = conv_b.reshape(n_lane_chunks, 1, LANES)
    y = _conv_layer(x2.reshape(batch, seq, d), gain[None, :], (0.5 * w_in).astype(BF16), cw, cb,
                    0.5 * ln_g[None, :], 0.5 * ln_b[None, :], w_out.astype(BF16))
    return y.reshape(batch * seq, d)


def kernel(x, a_norm, a_w_in, a_w_out, b_norm, b_w_in, b_v_ln_g, b_v_ln_b, b_w_s, b_b_s, b_w_out, c_norm, c_w_in, c_conv_w, c_conv_b, c_ln_g, c_ln_b, c_w_out, d_norm, d_w_in, d_b_f, d_w_out, final_norm):
    batch, seq, d = x.shape
    counts = (a_norm.shape[0], b_norm.shape[0], c_norm.shape[0], d_norm.shape[0])
    depth = sum(counts)
    fg = final_norm[None, :]
    x2 = x.reshape(batch * seq, d)
    for i in range(depth):
        kind, j = i % 4, i // 4
        last = i == depth - 1
        if kind == 0:
            x2 = _layer_stick_breaking(x2, batch, seq, a_norm[j], a_w_in[j], a_w_out[j], fg, last)
        elif kind == 1:
            x2 = _layer_gmlp(x2, b_norm[j], b_w_in[j], b_v_ln_g[j], b_v_ln_b[j], b_w_s[j], b_b_s[j], b_w_out[j])
        elif kind == 2:
            x2 = _layer_conv(x2, batch, seq, c_norm[j], c_w_in[j], c_conv_w[j], c_conv_b[j], c_ln_g[j],
                             c_ln_b[j], c_w_out[j])
        else:
            x2 = _layer_forgetting(x2, batch, seq, d_norm[j], d_w_in[j], d_b_f[j], d_w_out[j], fg, last)
    return x2.reshape(batch, seq, d)
```

```python
import functools
import math

import jax
import jax.numpy as jnp
from jax import lax
from jax.experimental import pallas as pl
from jax.experimental.pallas import tpu as pltpu

EPS = 1e-6
LANES = 128
HEAD_DIM = 64
GM_CHUNK = 128
GM_GROUPS = 16
CV_KERNEL = 31
HALO = 32
TQ = 256
TK = 256
PAIRS = 8
VMEM_LIMIT = 56 * 1024 * 1024
LOG2E = math.log2(math.e)
UNDERFLOW_LOG2 = -150.0
Q_SCALE = LOG2E / math.sqrt(HEAD_DIM)

F32 = jnp.float32
BF16 = jnp.bfloat16


def _resident(shape):
    nd = len(shape)
    return pl.BlockSpec(shape, lambda *_: (0,) * nd, pipeline_mode=pl.Buffered(1))


def _rms(x, gain):
    return x * lax.rsqrt(jnp.mean(x * x, axis=-1, keepdims=True) + EPS) * gain


def _silu_of_half(h):
    return h + h * jnp.tanh(h)


def _silu(x):
    return _silu_of_half(0.5 * x)


def _gelu_tanh(x):
    c = math.sqrt(2.0 / math.pi)
    return 0.5 * x * (1.0 + jnp.tanh(c * (x + 0.044715 * (x * x * x))))


def _log_sigmoid(x):
    return jnp.minimum(x, 0.0) - jnp.log(1.0 + jnp.exp(-jnp.abs(x)))


def _dot(a, b):
    return jnp.dot(a, b, preferred_element_type=F32)


def _dot_nt(a, b):
    return lax.dot_general(a, b, (((1,), (1,)), ((), ())), preferred_element_type=F32)


def _norm_proj_kernel(x_ref, gain_ref, w_ref, o_ref, *, n_chunk):
    hn = _rms(x_ref[...], gain_ref[...]).astype(BF16)
    for c in range(0, o_ref.shape[1], n_chunk):
        o_ref[:, c:c + n_chunk] = _dot(hn, w_ref[:, c:c + n_chunk]).astype(o_ref.dtype)


def _norm_proj_gate_kernel(x_ref, gain_ref, w_ref, wf_ref, bf_ref, o_ref, lf_ref, *, n_chunk):
    hn = _rms(x_ref[...], gain_ref[...]).astype(BF16)
    for c in range(0, o_ref.shape[1], n_chunk):
        o_ref[:, c:c + n_chunk] = _dot(hn, w_ref[:, c:c + n_chunk]).astype(o_ref.dtype)
    lf_ref[...] = _log_sigmoid(_dot(hn, wf_ref[...]) + bf_ref[...]) * LOG2E


def _norm_proj(x2, gain, w, tm=512, n_chunk=1024):
    n, d = x2.shape
    e = w.shape[1]
    return pl.pallas_call(
        functools.partial(_norm_proj_kernel, n_chunk=n_chunk),
        grid=(n // tm,),
        in_specs=[pl.BlockSpec((tm, d), lambda i: (i, 0)), _resident((1, d)), _resident((d, e))],
        out_specs=pl.BlockSpec((tm, e), lambda i: (i, 0)),
        out_shape=jax.ShapeDtypeStruct((n, e), BF16),
        compiler_params=pltpu.CompilerParams(dimension_semantics=("arbitrary",),
                                             vmem_limit_bytes=VMEM_LIMIT),
        name="norm_proj",
    )(x2, gain, w)


def _norm_proj_gate(x2, gain, w, wf, bf, tm=512, n_chunk=1024):
    n, d = x2.shape
    e = w.shape[1]
    return pl.pallas_call(
        functools.partial(_norm_proj_gate_kernel, n_chunk=n_chunk),
        grid=(n // tm,),
        in_specs=[pl.BlockSpec((tm, d), lambda i: (i, 0)), _resident((1, d)), _resident((d, e)),
                  _resident((d, LANES)), _resident((1, LANES))],
        out_specs=[pl.BlockSpec((tm, e), lambda i: (i, 0)),
                   pl.BlockSpec((tm, LANES), lambda i: (i, 0))],
        out_shape=[jax.ShapeDtypeStruct((n, e), BF16), jax.ShapeDtypeStruct((n, LANES), F32)],
        compiler_params=pltpu.CompilerParams(dimension_semantics=("arbitrary",),
                                             vmem_limit_bytes=VMEM_LIMIT),
        name="norm_proj_gate",
    )(x2, gain, w, wf, bf)


def _split3(x):
    x1 = x.astype(BF16)
    r1 = x - x1.astype(F32)
    x2 = r1.astype(BF16)
    x3 = (r1 - x2.astype(F32)).astype(BF16)
    return x1, x2, x3


def _cumsum_kernel(lf_ref, o_ref):
    s, h = lf_ref.shape
    row = lax.broadcasted_iota(jnp.int32, (TK, TK), 0)
    col = lax.broadcasted_iota(jnp.int32, (TK, TK), 1)
    lower = jnp.where(col <= row, 1.0, 0.0).astype(BF16)
    carry = jnp.zeros((1, h), F32)
    for c in range(0, s, TK):
        x = lf_ref[c:c + TK, :]
        x1, x2, x3 = _split3(x)
        cs = _dot(lower, x1) + _dot(lower, x2) + _dot(lower, x3)
        o_ref[c:c + TK, :] = cs + carry
        carry = carry + jnp.sum(x, axis=0, keepdims=True)


def _cumsum_seq(lf, seq):
    n, h = lf.shape
    return pl.pallas_call(
        _cumsum_kernel,
        grid=(n // seq,),
        in_specs=[pl.BlockSpec((seq, h), lambda b: (b, 0))],
        out_specs=pl.BlockSpec((seq, h), lambda b: (b, 0)),
        out_shape=jax.ShapeDtypeStruct((n, h), F32),
        compiler_params=pltpu.CompilerParams(dimension_semantics=("arbitrary",)),
        name="gate_cumsum",
    )(lf)


def _stack_heads(q):
    lane = lax.broadcasted_iota(jnp.int32, q.shape, 1)
    zero = jnp.zeros_like(q)
    return jnp.concatenate([jnp.where(lane < HEAD_DIM, q, zero), jnp.where(lane >= HEAD_DIM, q, zero)], axis=0)


def _unstack_heads(acc):
    tq = acc.shape[0] // 2
    lane = lax.broadcasted_iota(jnp.int32, (tq, LANES), 1)
    return jnp.where(lane < HEAD_DIM, acc[:tq], acc[tq:])


def _diag_mask(strict):
    r = lax.broadcasted_iota(jnp.int32, (2 * TQ, TK), 0)
    c = lax.broadcasted_iota(jnp.int32, (2 * TQ, TK), 1)
    t = jnp.where(r >= TQ, r - TQ, r)
    return c < t if strict else c <= t


def _pair(ref, g, rows):
    return ref[rows, g * LANES:(g + 1) * LANES]


def _halves(x):
    return x[:, :LANES], x[:, LANES:]


def _rowsum(x0, x1):
    return jnp.broadcast_to(jnp.sum(x0 + x1, axis=1, keepdims=True), x0.shape)


def _neg_abs(x):
    bits = lax.bitcast_convert_type(x, jnp.uint32) | jnp.uint32(0x80000000)
    return lax.bitcast_convert_type(bits, F32)


def _sb_attn_kernel(q_ref, k_ref, v_ref, tri_ref, g_ref, x_ref, wout_ref, fg_ref, out_ref,
                    q2_scr, og_scr, carry_scr, acc_scr, *, final_norm):
    qi = pl.program_id(1)
    for g in range(PAIRS):
        q2_scr[g] = _stack_heads(_pair(q_ref, g, slice(None)))
        carry_scr[g] = jnp.zeros((2 * TQ, LANES), F32)
        acc_scr[g] = jnp.zeros((2 * TQ, LANES), F32)

    def block(j, mask):
        rows = pl.ds(pl.multiple_of(j * TK, TK), TK)
        top = None
        for g in range(PAIRS):
            z = _dot_nt(q2_scr[g], _pair(k_ref, g, rows))
            log_beta =jnp.minimum(z, 0.0) - jnp.log2(1.0 + jnp.exp2(_neg_abs(z)))
            log_rem = log_beta - z
            if mask is not None:
                log_rem = jnp.where(mask, log_rem, 0.0)
            after = _dot(log_rem.astype(BF16), tri_ref[...])
            carry = carry_scr[g]
            t0, t1 = _halves(log_beta + after)
            w = jnp.exp2(jnp.concatenate([t0 + carry, t1 + carry], axis=1))
            if mask is not None:
                w = jnp.where(mask, w, 0.0)
            acc_scr[g] += _dot(w.astype(BF16), _pair(v_ref, g, rows))
            carry = carry + _rowsum(*_halves(log_rem))
            carry_scr[g] = carry
            top_g = jnp.max(carry, axis=0, keepdims=True)
            top = top_g if top is None else jnp.maximum(top, top_g)
        return jnp.max(top)

    top = block(qi, _diag_mask(strict=True))

    def cond(state):
        return jnp.logical_and(state[0] < qi, state[1] > UNDERFLOW_LOG2)

    def body(state):
        return state[0] + 1, block(qi - 1 - state[0], None)

    lax.while_loop(cond, body, (jnp.int32(0), top))
    _gated_out(lambda g: _unstack_heads(acc_scr[g]), g_ref, x_ref, wout_ref, fg_ref, out_ref, og_scr, final_norm)


def _gated_out(attn_out, g_ref, x_ref, wout_ref, fg_ref, out_ref, og_scr, final_norm):
    for g in range(PAIRS):
        cols = slice(g * LANES, (g + 1) * LANES)
        og_scr[:, cols] = (attn_out(g) * _silu(g_ref[:, cols].astype(F32))).astype(BF16)
    y = x_ref[...] + _dot(og_scr[...], wout_ref[...])
    if final_norm:
        y = _rms(y, fg_ref[...])
    out_ref[...] = y


def _fox_attn_kernel(q_ref, k_ref, v_ref, cum_ref, g_ref, x_ref, wout_ref, fg_ref, out_ref,
                     q2_scr, og_scr, m_scr, l_scr, acc_scr, *, final_norm):
    qi = pl.program_id(1)
    for g in range(PAIRS):
        q2_scr[g] = _stack_heads(_pair(q_ref, g, slice(None)))
        m_scr[g] = jnp.full((2 * TQ, LANES), -jnp.inf, F32)
        l_scr[g] = jnp.zeros((2 * TQ, LANES), F32)
        acc_scr[g] = jnp.zeros((2 * TQ, LANES), F32)

    def blocks(js, mask):
        rows = [pl.ds(pl.multiple_of(j * TK, TK), TK) for j in js]
        for g in range(PAIRS):
            halves = []
            for j, r in zip(js, rows):
                cs = cum_ref[g, j]
                bias = jnp.concatenate([jnp.broadcast_to(cs[0:1], (TQ, TK)),
                                        jnp.broadcast_to(cs[1:2], (TQ, TK))], axis=0)
                s = _dot_nt(q2_scr[g], _pair(k_ref, g, r)) - bias
                if mask is not None:
                    s = jnp.where(mask, s, -jnp.inf)
                halves.extend(_halves(s))
            smax = functools.reduce(jnp.maximum, halves)
            m_old = m_scr[g]
            m_new = jnp.maximum(m_old, jnp.max(smax, axis=1, keepdims=True))
            alpha = jnp.exp2(m_old - m_new)
            ps = [jnp.exp2(h - m_new) for h in halves]
            psum = functools.reduce(jnp.add, ps)
            l_scr[g] = alpha * l_scr[g] + jnp.broadcast_to(jnp.sum(psum, axis=1, keepdims=True), psum.shape)
            acc = alpha * acc_scr[g]
            for i, r in enumerate(rows):
                p = jnp.concatenate(ps[2 * i:2 * i + 2], axis=1).astype(BF16)
                acc = acc + _dot(p, _pair(v_ref, g, r))
            acc_scr[g] = acc
            m_scr[g] = m_new

    blocks([qi], _diag_mask(strict=False))

    def body(i, _):
        j = qi - 1 - 2 * i
        blocks([j, j - 1], None)
        return 0

    lax.fori_loop(0, qi // 2, body, 0)

    @pl.when(qi % 2 == 1)
    def _():
        blocks([0], None)

    _gated_out(lambda g: _unstack_heads(acc_scr[g] / l_scr[g]), g_ref, x_ref, wout_ref, fg_ref, out_ref, og_scr,
               final_norm)


def _attention_layer(kernel_fn, n_stats, proj, x2, w_out, final_gain, final_norm, batch, seq, extra, extra_spec,
                     name):
    width, d = w_out.shape
    assert width == PAIRS * LANES
    proj3 = proj.reshape(batch, seq, proj.shape[1])
    col = lambda c: pl.BlockSpec((None, TQ, width), lambda b, i: (b, i, c))
    seq_col = lambda c: pl.BlockSpec((None, seq, width), lambda b, i: (b, 0, c))
    return pl.pallas_call(
        functools.partial(kernel_fn, final_norm=final_norm),
        grid=(batch, seq // TQ),
        in_specs=[col(0), seq_col(1), seq_col(2), extra_spec, col(3),
                  pl.BlockSpec((None, TQ, d), lambda b, i: (b, i, 0)), _resident((width, d)), _resident((1, d))],
        out_specs=pl.BlockSpec((None, TQ, d), lambda b, i: (b, i, 0)),
        out_shape=jax.ShapeDtypeStruct((batch, seq, d), F32),
        scratch_shapes=[pltpu.VMEM((PAIRS, 2 * TQ, LANES), BF16), pltpu.VMEM((TQ, width), BF16)]
        + [pltpu.VMEM((PAIRS, 2 * TQ, LANES), F32)] * n_stats,
        compiler_params=pltpu.CompilerParams(dimension_semantics=("arbitrary", "arbitrary"),
                                             vmem_limit_bytes=VMEM_LIMIT),
        name=name,
    )(proj3, proj3, proj3, extra, proj3, x2.reshape(batch, seq, d), w_out, final_gain).reshape(batch * seq, d)


def _gmlp_kernel(x_ref, gain_ref, win_ref, lng_ref, lnb_ref, ws_ref, bias_ref, wout_ref, out_ref,
                 v_scr, u_scr, g_scr, vn_scr, s_scr, *, n_chunk):
    tm = x_ref.shape[0]
    width = wout_ref.shape[0]
    gw = width // GM_GROUPS
    hn = _rms(x_ref[...], gain_ref[...]).astype(BF16)

    for c in range(0, width, n_chunk):
        v_scr[:, c:c + n_chunk] = _gelu_tanh(_dot(hn, win_ref[:, width + c:width + c + n_chunk]))
    for c in range(0, width, n_chunk):
        u_scr[:, c:c + n_chunk] = _dot(hn, win_ref[:, c:c + n_chunk])
        g_scr[:, c:c + n_chunk] = _dot(hn, win_ref[:, 2 * width + c:2 * width + c + n_chunk])
    v = v_scr[...]
    mu = jnp.mean(v, axis=-1, keepdims=True)
    vc = v - mu
    var = jnp.mean(vc * vc, axis=-1, keepdims=True)
    vn_scr[...] = (vc * lax.rsqrt(var + EPS) * lng_ref[...] + lnb_ref[...]).astype(BF16)

    row = lax.broadcasted_iota(jnp.int32, (GM_CHUNK, GM_CHUNK), 0)
    col = lax.broadcasted_iota(jnp.int32, (GM_CHUNK, GM_CHUNK), 1)
    for g in range(GM_GROUPS):
        wsm = jnp.where(row >= col, ws_ref[g], 0.0).astype(BF16)
        for n in range(0, tm, GM_CHUNK):
            s_scr[n:n + GM_CHUNK, g * gw:(g + 1) * gw] = (
                _dot(wsm, vn_scr[n:n + GM_CHUNK, g * gw:(g + 1) * gw]) + bias_ref[:, g * gw:(g + 1) * gw])

    y = x_ref[...]
    for c in range(0, width, n_chunk):
        u = _gelu_tanh(u_scr[:, c:c + n_chunk])
        og = ((u * s_scr[:, c:c + n_chunk]) * _silu(g_scr[:, c:c + n_chunk])).astype(BF16)
        y = y + _dot(og, wout_ref[c:c + n_chunk, :])
    out_ref[...] = y


def _gmlp_layer(x2, gain, win, lng, lnb, ws, bias_full, wout, tm=512, n_chunk=512):
    n, d = x2.shape
    width = wout.shape[0]
    return pl.pallas_call(
        functools.partial(_gmlp_kernel, n_chunk=n_chunk),
        grid=(n // tm,),
        in_specs=[pl.BlockSpec((tm, d), lambda i: (i, 0)), _resident((1, d)), _resident(win.shape),
                  _resident((1, width)), _resident((1, width)), _resident(ws.shape),
                  _resident(bias_full.shape), _resident(wout.shape)],
        out_specs=pl.BlockSpec((tm, d), lambda i: (i, 0)),
        out_shape=jax.ShapeDtypeStruct((n, d), F32),
        scratch_shapes=[pltpu.VMEM((tm, width), F32), pltpu.VMEM((tm, width), F32), pltpu.VMEM((tm, width), F32),
                        pltpu.VMEM((tm, width), BF16), pltpu.VMEM((tm, width), F32)],
        compiler_params=pltpu.CompilerParams(dimension_semantics=("arbitrary",),
                                             vmem_limit_bytes=VMEM_LIMIT),
        name="gmlp_layer",
    )(x2, gain, win, lng, lnb, ws, bias_full, wout)


def _conv_kernel(x_ref, gain_ref, win_ref, cw_ref, cb_ref, lng_ref, lnb_ref, wout_ref, out_ref,
                 y_scr, c_scr, g_scr, og_scr, *, sub, n_chunk, row_chunk):
    tm = x_ref.shape[0]
    width = wout_ref.shape[0]
    n_lane_chunks = width // LANES
    t_idx = pl.program_id(1)

    @pl.when(t_idx == 0)
    def _():
        y_scr[:, 0:HALO, :] = jnp.zeros((n_lane_chunks, HALO, LANES), F32)

    starts = list(range(0, tm, sub))
    hns = [_rms(x_ref[r0:r0 + sub, :], gain_ref[...]).astype(BF16) for r0 in starts]
    for c in range(0, width, n_chunk):
        chunk_ids = [c // LANES + i for i in range(n_chunk // LANES)]
        for r0, hn in zip(starts, hns):
            a = _dot(hn, win_ref[:, c:c + n_chunk])
            b = _dot(hn, win_ref[:, width + c:width + c + n_chunk])
            y = a + a * jnp.tanh(b)
            g_scr[r0:r0 + sub, c:c + n_chunk] = _dot(hn, win_ref[:, 2 * width + c:2 * width + c + n_chunk])
            for i, ci in enumerate(chunk_ids):
                y_scr[ci, HALO + r0:HALO + r0 + sub, :] = y[:, i * LANES:(i + 1) * LANES]
        for ci in chunk_ids:
            for r in range(0, tm, row_chunk):
                acc = jnp.zeros((row_chunk, LANES), F32)
                for k in range(CV_KERNEL):
                    acc = acc + cw_ref[ci, k:k + 1, :] * y_scr[ci, HALO - (CV_KERNEL - 1) + k + r:
                                                               HALO - (CV_KERNEL - 1) + k + r + row_chunk, :]
                c_scr[ci, r:r + row_chunk, :] = acc + cb_ref[ci]
            y_scr[ci, 0:HALO, :] = y_scr[ci, tm:tm + HALO, :]

    for r0 in starts:
        rows = slice(r0, r0 + sub)
        tot = c_scr[0, rows, :]
        for ci in range(1, n_lane_chunks):
            tot = tot + c_scr[ci, rows, :]
        mu = jnp.sum(tot, axis=-1, keepdims=True) * (1.0 / width)
        d0 = c_scr[0, rows, :] - mu
        sq = d0 * d0
        for ci in range(1, n_lane_chunks):
            dc = c_scr[ci, rows, :] - mu
            sq = sq + dc * dc
        rstd = lax.rsqrt(jnp.sum(sq, axis=-1, keepdims=True) * (1.0 / width) + EPS)

        for ci in range(n_lane_chunks):
            lo = ci * LANES
            yh = (c_scr[ci, rows, :] - mu) * rstd * lng_ref[:, lo:lo + LANES] + lnb_ref[:, lo:lo + LANES]
            og_scr[rows, lo:lo + LANES] = (_silu_of_half(yh)
                                           * _silu_of_half(g_scr[rows, lo:lo + LANES])).astype(BF16)

        out_ref[rows, :] = x_ref[rows, :] + _dot(og_scr[rows, :], wout_ref[...])


def _conv_layer(x3, gain, win, cw, cb, lng, lnb, wout, tm=512, sub=256, n_chunk=256, row_chunk=128):
    batch, seq, d = x3.shape
    width = wout.shape[0]
    n_lane_chunks = width // LANES
    return pl.pallas_call(
        functools.partial(_conv_kernel, sub=sub, n_chunk=n_chunk, row_chunk=row_chunk),
        grid=(batch, seq // tm),
        in_specs=[pl.BlockSpec((None, tm, d), lambda b, t: (b, t, 0)), _resident((1, d)), _resident(win.shape),
                  _resident(cw.shape), _resident(cb.shape), _resident((1, width)), _resident((1, width)),
                  _resident(wout.shape)],
        out_specs=pl.BlockSpec((None, tm, d), lambda b, t: (b, t, 0)),
        out_shape=jax.ShapeDtypeStruct((batch, seq, d), F32),
        scratch_shapes=[pltpu.VMEM((n_lane_chunks, tm + HALO, LANES), F32),
                        pltpu.VMEM((n_lane_chunks, tm, LANES), F32),
                        pltpu.VMEM((tm, width), F32),
                        pltpu.VMEM((tm, width), BF16)],
        compiler_params=pltpu.CompilerParams(dimension_semantics=("arbitrary", "arbitrary"),
                                             vmem_limit_bytes=VMEM_LIMIT),
        name="conv_layer",
    )(x3, gain, win, cw, cb, lng, lnb, wout)


def _scale_q_columns(w_in, width):
    col = lax.broadcasted_iota(jnp.int32, (1, w_in.shape[1]), 1)
    return w_in * jnp.where(col < width, Q_SCALE, 1.0)


def _layer_stick_breaking(x2, batch, seq, gain, w_in, w_out, final_gain, final_norm):
    width = w_out.shape[0]
    proj = _norm_proj(x2, gain[None, :], _scale_q_columns(w_in, width).astype(BF16))
    row = lax.broadcasted_iota(jnp.int32, (TK, TK), 0)
    col = lax.broadcasted_iota(jnp.int32, (TK, TK), 1)
    tri = (row > col).astype(BF16)
    return _attention_layer(_sb_attn_kernel, 2, proj, x2, w_out.astype(BF16), final_gain, final_norm, batch, seq,
                            tri, _resident((TK, TK)), "sb_attention")


def _layer_forgetting(x2, batch, seq, gain, w_in, b_f, w_out, final_gain, final_norm):
    width = w_out.shape[0]
    n_heads = b_f.shape[0]
    d = w_in.shape[0]
    w_main = _scale_q_columns(w_in[:, :4 * width], width).astype(BF16)
    w_f = jnp.zeros((d, LANES), BF16).at[:, :n_heads].set(w_in[:, 4 * width:].astype(BF16))
    b_pad = jnp.zeros((1, LANES), F32).at[0, :n_heads].set(b_f)
    proj, lf = _norm_proj_gate(x2, gain[None, :], w_main, w_f, b_pad)
    cum = _cumsum_seq(lf, seq)
    n_pairs = n_heads // 2
    cum = cum[:, :n_heads].reshape(batch, seq // TK, TK, n_pairs, 2).transpose(0, 3, 1, 4, 2)
    cum_spec = pl.BlockSpec((None, PAIRS, seq // TK, 2, TK), lambda b, i: (b, 0, 0, 0, 0))
    return _attention_layer(_fox_attn_kernel, 3, proj, x2, w_out.astype(BF16), final_gain, final_norm, batch, seq,
                            cum, cum_spec, "fox_attention")


def _layer_gmlp(x2, gain, w_in, ln_g, ln_b, w_s, b_s, w_out):
    width = w_out.shape[0]
    bias_full = jnp.repeat(b_s.T, width // GM_GROUPS, axis=1)
    return _gmlp_layer(x2, gain[None, :], w_in.astype(BF16), ln_g[None, :], ln_b[None, :], w_s, bias_full,
                       w_out.astype(BF16))


def _layer_conv(x2, batch, seq, gain, w_in, conv_w, conv_b, ln_g, ln_b, w_out):
    width = w_out.shape[0]
    n_lane_chunks = width // LANES
    d = x2.shape[1]
    cw = jnp.zeros((HALO, width), F32).at[:CV_KERNEL].set(conv_w)
    cw = cw.reshape(HALO, n_lane_chunks, LANES).transpose(1, 0, 2)
    cb = conv_b.reshape(n_lane_chunks, 1, LANES)
    y = _conv_layer(x2.reshape(batch, seq, d), gain[None, :], (0.5 * w_in).astype(BF16), cw, cb,
                    0.5 * ln_g[None, :], 0.5 * ln_b[None, :], w_out.astype(BF16))
    return y.reshape(batch * seq, d)


def kernel(x, a_norm, a_w_in, a_w_out, b_norm, b_w_in, b_v_ln_g, b_v_ln_b, b_w_s, b_b_s, b_w_out, c_norm, c_w_in, c_conv_w, c_conv_b, c_ln_g, c_ln_b, c_w_out, d_norm, d_w_in, d_b_f, d_w_out, final_norm):
    batch, seq, d = x.shape
    counts = (a_norm.shape[0], b_norm.shape[0], c_norm.shape[0], d_norm.shape[0])
    depth = sum(counts)
    fg = final_norm[None, :]
    x2 = x.reshape(batch * seq, d)
    for i in range(depth):
        kind, j = i % 4, i // 4
        last = i == depth - 1
        if kind == 0:
            x2 = _layer_stick_breaking(x2, batch, seq, a_norm[j], a_w_in[j], a_w_out[j], fg, last)
        elif kind == 1:
            x2 = _layer_gmlp(x2, b_norm[j], b_w_in[j], b_v_ln_g[j], b_v_ln_b[j], b_w_s[j], b_b_s[j], b_w_out[j])
        elif kind == 2:
            x2 = _layer_conv(x2, batch, seq, c_norm[j], c_w_in[j], c_conv_w[j], c_conv_b[j], c_ln_g[j],
                             c_ln_b[j], c_w_out[j])
        else:
            x2 = _layer_forgetting(x2, batch, seq, d_norm[j], d_w_in[j], d_b_f[j], d_w_out[j], fg, last)
    return x2.reshape(batch, seq, d)
```

```python
import functools
import math

import jax
import jax.numpy as jnp
from jax import lax
from jax.experimental import pallas as pl
from jax.experimental.pallas import tpu as pltpu

EPS = 1e-6
LANES = 128
HEAD_DIM = 64
GM_CHUNK = 128
GM_GROUPS = 16
CV_KERNEL = 31
HALO = 32
TQ = 256
TK = 256
PAIRS = 8
VMEM_LIMIT = 56 * 1024 * 1024
LOG2E = math.log2(math.e)
UNDERFLOW_LOG2 = -150.0
Q_SCALE = LOG2E / math.sqrt(HEAD_DIM)

F32 = jnp.float32
BF16 = jnp.bfloat16


def _resident(shape):
    nd = len(shape)
    return pl.BlockSpec(shape, lambda *_: (0,) * nd, pipeline_mode=pl.Buffered(1))


def _rms(x, gain):
    return x * lax.rsqrt(jnp.mean(x * x, axis=-1, keepdims=True) + EPS) * gain


def _silu_of_half(h):
    return h + h * jnp.tanh(h)


def _silu(x):
    return _silu_of_half(0.5 * x)


def _gelu_tanh(x):
    c = math.sqrt(2.0 / math.pi)
    return 0.5 * x * (1.0 + jnp.tanh(c * (x + 0.044715 * (x * x * x))))


def _log_sigmoid(x):
    return jnp.minimum(x, 0.0) - jnp.log(1.0 + jnp.exp(-jnp.abs(x)))


def _dot(a, b):
    return jnp.dot(a, b, preferred_element_type=F32)


def _dot_nt(a, b):
    return lax.dot_general(a, b, (((1,), (1,)), ((), ())), preferred_element_type=F32)


def _norm_proj_kernel(x_ref, gain_ref, w_ref, o_ref, *, n_chunk):
    hn = _rms(x_ref[...], gain_ref[...]).astype(BF16)
    for c in range(0, o_ref.shape[1], n_chunk):
        o_ref[:, c:c + n_chunk] = _dot(hn, w_ref[:, c:c + n_chunk]).astype(o_ref.dtype)


def _norm_proj_gate_kernel(x_ref, gain_ref, w_ref, wf_ref, bf_ref, o_ref, lf_ref, *, n_chunk):
    hn = _rms(x_ref[...], gain_ref[...]).astype(BF16)
    for c in range(0, o_ref.shape[1], n_chunk):
        o_ref[:, c:c + n_chunk] = _dot(hn, w_ref[:, c:c + n_chunk]).astype(o_ref.dtype)
    lf_ref[...] = _log_sigmoid(_dot(hn, wf_ref[...]) + bf_ref[...]) * LOG2E


def _norm_proj(x2, gain, w, tm=512, n_chunk=1024):
    n, d = x2.shape
    e = w.shape[1]
    return pl.pallas_call(
        functools.partial(_norm_proj_kernel, n_chunk=n_chunk),
        grid=(n // tm,),
        in_specs=[pl.BlockSpec((tm, d), lambda i: (i, 0)), _resident((1, d)), _resident((d, e))],
        out_specs=pl.BlockSpec((tm, e), lambda i: (i, 0)),
        out_shape=jax.ShapeDtypeStruct((n, e), BF16),
        compiler_params=pltpu.CompilerParams(dimension_semantics=("arbitrary",),
                                             vmem_limit_bytes=VMEM_LIMIT),
        name="norm_proj",
    )(x2, gain, w)


def _norm_proj_gate(x2, gain, w, wf, bf, tm=512, n_chunk=1024):
    n, d = x2.shape
    e = w.shape[1]
    return pl.pallas_call(
        functools.partial(_norm_proj_gate_kernel, n_chunk=n_chunk),
        grid=(n // tm,),
        in_specs=[pl.BlockSpec((tm, d), lambda i: (i, 0)), _resident((1, d)), _resident((d, e)),
                  _resident((d, LANES)), _resident((1, LANES))],
        out_specs=[pl.BlockSpec((tm, e), lambda i: (i, 0)),
                   pl.BlockSpec((tm, LANES), lambda i: (i, 0))],
        out_shape=[jax.ShapeDtypeStruct((n, e), BF16), jax.ShapeDtypeStruct((n, LANES), F32)],
        compiler_params=pltpu.CompilerParams(dimension_semantics=("arbitrary",),
                                             vmem_limit_bytes=VMEM_LIMIT),
        name="norm_proj_gate",
    )(x2, gain, w, wf, bf)


def _split3(x):
    x1 = x.astype(BF16)
    r1 = x - x1.astype(F32)
    x2 = r1.astype(BF16)
    x3 = (r1 - x2.astype(F32)).astype(BF16)
    return x1, x2, x3


def _cumsum_kernel(lf_ref, o_ref):
    s, h = lf_ref.shape
    row = lax.broadcasted_iota(jnp.int32, (TK, TK), 0)
    col = lax.broadcasted_iota(jnp.int32, (TK, TK), 1)
    lower = jnp.where(col <= row, 1.0, 0.0).astype(BF16)
    carry = jnp.zeros((1, h), F32)
    for c in range(0, s, TK):
        x = lf_ref[c:c + TK, :]
        x1, x2, x3 = _split3(x)
        cs = _dot(lower, x1) + _dot(lower, x2) + _dot(lower, x3)
        o_ref[c:c + TK, :] = cs + carry
        carry = carry + jnp.sum(x, axis=0, keepdims=True)


def _cumsum_seq(lf, seq):
    n, h = lf.shape
    return pl.pallas_call(
        _cumsum_kernel,
        grid=(n // seq,),
        in_specs=[pl.BlockSpec((seq, h), lambda b: (b, 0))],
        out_specs=pl.BlockSpec((seq, h), lambda b: (b, 0)),
        out_shape=jax.ShapeDtypeStruct((n, h), F32),
        compiler_params=pltpu.CompilerParams(dimension_semantics=("arbitrary",)),
        name="gate_cumsum",
    )(lf)


def _stack_heads(q):
    lane = lax.broadcasted_iota(jnp.int32, q.shape, 1)
    zero = jnp.zeros_like(q)
    return jnp.concatenate([jnp.where(lane < HEAD_DIM, q, zero), jnp.where(lane >= HEAD_DIM, q, zero)], axis=0)


def _unstack_heads(acc):
    tq = acc.shape[0] // 2
    lane = lax.broadcasted_iota(jnp.int32, (tq, LANES), 1)
    return jnp.where(lane < HEAD_DIM, acc[:tq], acc[tq:])


def _diag_mask(strict):
    r = lax.broadcasted_iota(jnp.int32, (2 * TQ, TK), 0)
    c = lax.broadcasted_iota(jnp.int32, (2 * TQ, TK), 1)
    t = jnp.where(r >= TQ, r - TQ, r)
    return c < t if strict else c <= t


def _pair(ref, g, rows):
    return ref[rows, g * LANES:(g + 1) * LANES]


def _halves(x):
    return x[:, :LANES], x[:, LANES:]


def _rowsum(x0, x1):
    return jnp.broadcast_to(jnp.sum(x0 + x1, axis=1, keepdims=True), x0.shape)


def _neg_abs(x):
    bits = lax.bitcast_convert_type(x, jnp.uint32) | jnp.uint32(0x80000000)
    return lax.bitcast_convert_type(bits, F32)


def _sb_attn_kernel(q_ref, k_ref, v_ref, tri_ref, g_ref, x_ref, wout_ref, fg_ref, out_ref,
                    q2_scr, og_scr, carry_scr, acc_scr, *, final_norm):
    qi = pl.program_id(1)
    for g in range(PAIRS):
        q2_scr[g] = _stack_heads(_pair(q_ref, g, slice(None)))
        carry_scr[g] = jnp.zeros((2 * TQ, LANES), F32)
        acc_scr[g] = jnp.zeros((2 * TQ, LANES), F32)

    def blocks(js, masks):
        rows = [pl.ds(pl.multiple_of(j * TK, TK), TK) for j in js]
        top = None
        for g in range(PAIRS):
            log_betas, log_rems = [], []
            for r, mask in zip(rows, masks):
                z = _dot_nt(q2_scr[g], _pair(k_ref, g, r))
                log_beta = jnp.minimum(z, 0.0) - jnp.log2(1.0 + jnp.exp2(_neg_abs(z)))
                log_rem = log_beta - z
                if mask is not None:
                    log_rem = jnp.where(mask, log_rem, 0.0)
                log_betas.append(log_beta)
                log_rems.append(log_rem)
            afters = [_dot(lr.astype(BF16), tri_ref[...]) for lr in log_rems]
            carry = carry_scr[g]
            acc = acc_scr[g]
            for r, mask, log_beta, log_rem, after in zip(rows, masks, log_betas, log_rems, afters):
                t0, t1 = _halves(log_beta + after)
                w = jnp.exp2(jnp.concatenate([t0 + carry, t1 + carry], axis=1))
                if mask is not None:
                    w = jnp.where(mask, w, 0.0)
                acc = acc + _dot(w.astype(BF16), _pair(v_ref, g, r))
                carry = carry + _rowsum(*_halves(log_rem))
            acc_scr[g] = acc
            carry_scr[g] = carry
            top_g = jnp.max(carry, axis=0, keepdims=True)
            top = top_g if top is None else jnp.maximum(top, top_g)
        return jnp.max(top)

    diag = _diag_mask(strict=True)
    top = lax.cond(qi > 0, lambda: blocks([qi, qi - 1], [diag, None]), lambda: blocks([qi], [diag]))

    def cond(state):
        return jnp.logical_and(state[0] < qi, state[1] > UNDERFLOW_LOG2)

    def body(state):
        return state[0] + 1, blocks([qi - 1 - state[0]], [None])

    lax.while_loop(cond, body, (jnp.minimum(qi, 1), top))
    _gated_out(lambda g: _unstack_heads(acc_scr[g]), g_ref, x_ref, wout_ref, fg_ref, out_ref, og_scr, final_norm)


def _gated_out(attn_out, g_ref, x_ref, wout_ref, fg_ref, out_ref, og_scr, final_norm):
    for g in range(PAIRS):
        cols = slice(g * LANES, (g + 1) * LANES)
        og_scr[:, cols] = (attn_out(g) * _silu(g_ref[:, cols].astype(F32))).astype(BF16)
    y = x_ref[...] + _dot(og_scr[...], wout_ref[...])
    if final_norm:
        y = _rms(y, fg_ref[...])
    out_ref[...] = y


def _fox_attn_kernel(q_ref, k_ref, v_ref, cum_ref, g_ref, x_ref, wout_ref, fg_ref, out_ref,
                     q2_scr, og_scr, m_scr, l_scr, acc_scr, *, final_norm):
    qi = pl.program_id(1)
    for g in range(PAIRS):
        q2_scr[g] = _stack_heads(_pair(q_ref, g, slice(None)))
        m_scr[g] = jnp.full((2 * TQ, LANES), -jnp.inf, F32)
        l_scr[g] = jnp.zeros((2 * TQ, LANES), F32)
        acc_scr[g] = jnp.zeros((2 * TQ, LANES), F32)

    def blocks(js, mask):
        rows = [pl.ds(pl.multiple_of(j * TK, TK), TK) for j in js]
        for g in range(PAIRS):
            halves = []
            for j, r in zip(js, rows):
                cs = cum_ref[g, j]
                bias = jnp.concatenate([jnp.broadcast_to(cs[0:1], (TQ, TK)),
                                        jnp.broadcast_to(cs[1:2], (TQ, TK))], axis=0)
                s = _dot_nt(q2_scr[g], _pair(k_ref, g, r)) - bias
                if mask is not None:
                    s = jnp.where(mask, s, -jnp.inf)
                halves.extend(_halves(s))
            smax = functools.reduce(jnp.maximum, halves)
            m_old = m_scr[g]
            m_new = jnp.maximum(m_old, jnp.max(smax, axis=1, keepdims=True))
            alpha = jnp.exp2(m_old - m_new)
            ps = [jnp.exp2(h - m_new) for h in halves]
            psum = functools.reduce(jnp.add, ps)
            l_scr[g] = alpha * l_scr[g] + jnp.broadcast_to(jnp.sum(psum, axis=1, keepdims=True), psum.shape)
            acc = alpha * acc_scr[g]
            for i, r in enumerate(rows):
                p = jnp.concatenate(ps[2 * i:2 * i + 2], axis=1).astype(BF16)
                acc = acc + _dot(p, _pair(v_ref, g, r))
            acc_scr[g] = acc
            m_scr[g] = m_new

    blocks([qi], _diag_mask(strict=False))

    def body(i, _):
        j = qi - 1 - 2 * i
        blocks([j, j - 1], None)
        return 0

    lax.fori_loop(0, qi // 2, body, 0)

    @pl.when(qi % 2 == 1)
    def _():
        blocks([0], None)

    _gated_out(lambda g: _unstack_heads(acc_scr[g] / l_scr[g]), g_ref, x_ref, wout_ref, fg_ref, out_ref, og_scr,
               final_norm)


def _attention_layer(kernel_fn, n_stats, proj, x2, w_out, final_gain, final_norm, batch, seq, extra, extra_spec,
                     name):
    width, d = w_out.shape
    assert width == PAIRS * LANES
    proj3 = proj.reshape(batch, seq, proj.shape[1])
    col = lambda c: pl.BlockSpec((None, TQ, width), lambda b, i: (b, i, c))
    seq_col = lambda c: pl.BlockSpec((None, seq, width), lambda b, i: (b, 0, c))
    return pl.pallas_call(
        functools.partial(kernel_fn, final_norm=final_norm),
        grid=(batch, seq // TQ),
        in_specs=[col(0), seq_col(1), seq_col(2), extra_spec, col(3),
                  pl.BlockSpec((None, TQ, d), lambda b, i: (b, i, 0)), _resident((width, d)), _resident((1, d))],
        out_specs=pl.BlockSpec((None, TQ, d), lambda b, i: (b, i, 0)),
        out_shape=jax.ShapeDtypeStruct((batch, seq, d), F32),
        scratch_shapes=[pltpu.VMEM((PAIRS, 2 * TQ, LANES), BF16), pltpu.VMEM((TQ, width), BF16)]
        + [pltpu.VMEM((PAIRS, 2 * TQ, LANES), F32)] * n_stats,
        compiler_params=pltpu.CompilerParams(dimension_semantics=("arbitrary", "arbitrary"),
                                             vmem_limit_bytes=VMEM_LIMIT),
        name=name,
    )(proj3, proj3, proj3, extra, proj3, x2.reshape(batch, seq, d), w_out, final_gain).reshape(batch * seq, d)


def _gmlp_kernel(x_ref, gain_ref, win_ref, lng_ref, lnb_ref, ws_ref, bias_ref, wout_ref, out_ref,
                 v_scr, u_scr, g_scr, vn_scr, s_scr, *, n_chunk):
    tm = x_ref.shape[0]
    width = wout_ref.shape[0]
    gw = width // GM_GROUPS
    hn = _rms(x_ref[...], gain_ref[...]).astype(BF16)

    for c in range(0, width, n_chunk):
        v_scr[:, c:c + n_chunk] = _gelu_tanh(_dot(hn, win_ref[:, width + c:width + c + n_chunk]))
    for c in range(0, width, n_chunk):
        u_scr[:, c:c + n_chunk] = _dot(hn, win_ref[:, c:c + n_chunk])
        g_scr[:, c:c + n_chunk] = _dot(hn, win_ref[:, 2 * width + c:2 * width + c + n_chunk])
    v = v_scr[...]
    mu = jnp.mean(v, axis=-1, keepdims=True)
    vc = v - mu
    var = jnp.mean(vc * vc, axis=-1, keepdims=True)
    vn_scr[...] = (vc * lax.rsqrt(var + EPS) * lng_ref[...] + lnb_ref[...]).astype(BF16)

    row = lax.broadcasted_iota(jnp.int32, (GM_CHUNK, GM_CHUNK), 0)
    col = lax.broadcasted_iota(jnp.int32, (GM_CHUNK, GM_CHUNK), 1)
    for g in range(GM_GROUPS):
        wsm = jnp.where(row >= col, ws_ref[g], 0.0).astype(BF16)
        for n in range(0, tm, GM_CHUNK):
            s_scr[n:n + GM_CHUNK, g * gw:(g + 1) * gw] = (
                _dot(wsm, vn_scr[n:n + GM_CHUNK, g * gw:(g + 1) * gw]) + bias_ref[:, g * gw:(g + 1) * gw])

    y = x_ref[...]
    for c in range(0, width, n_chunk):
        u = _gelu_tanh(u_scr[:, c:c + n_chunk])
        og = ((u * s_scr[:, c:c + n_chunk]) * _silu(g_scr[:, c:c + n_chunk])).astype(BF16)
        y = y + _dot(og, wout_ref[c:c + n_chunk, :])
    out_ref[...] = y


def _gmlp_layer(x2, gain, win, lng, lnb, ws, bias_full, wout, tm=512, n_chunk=512):
    n, d = x2.shape
    width = wout.shape[0]
    return pl.pallas_call(
        functools.partial(_gmlp_kernel, n_chunk=n_chunk),
        grid=(n // tm,),
        in_specs=[pl.BlockSpec((tm, d), lambda i: (i, 0)), _resident((1, d)), _resident(win.shape),
                  _resident((1, width)), _resident((1, width)), _resident(ws.shape),
                  _resident(bias_full.shape), _resident(wout.shape)],
        out_specs=pl.BlockSpec((tm, d), lambda i: (i, 0)),
        out_shape=jax.ShapeDtypeStruct((n, d), F32),
        scratch_shapes=[pltpu.VMEM((tm, width), F32), pltpu.VMEM((tm, width), F32), pltpu.VMEM((tm, width), F32),
                        pltpu.VMEM((tm, width), BF16), pltpu.VMEM((tm, width), F32)],
        compiler_params=pltpu.CompilerParams(dimension_semantics=("arbitrary",),
                                             vmem_limit_bytes=VMEM_LIMIT),
        name="gmlp_layer",
    )(x2, gain, win, lng, lnb, ws, bias_full, wout)


def _conv_kernel(x_ref, gain_ref, win_ref, cw_ref, cb_ref, lng_ref, lnb_ref, wout_ref, out_ref,
                 y_scr, c_scr, g_scr, og_scr, *, sub, n_chunk, row_chunk):
    tm = x_ref.shape[0]
    width = wout_ref.shape[0]
    n_lane_chunks = width // LANES
    t_idx = pl.program_id(1)

    @pl.when(t_idx == 0)
    def _():
        y_scr[:, 0:HALO, :] = jnp.zeros((n_lane_chunks, HALO, LANES), F32)

    starts = list(range(0, tm, sub))
    hns = [_rms(x_ref[r0:r0 + sub, :], gain_ref[...]).astype(BF16) for r0 in starts]
    for c in range(0, width, n_chunk):
        chunk_ids = [c // LANES + i for i in range(n_chunk // LANES)]
        for r0, hn in zip(starts, hns):
            a = _dot(hn, win_ref[:, c:c + n_chunk])
            b = _dot(hn, win_ref[:, width + c:width + c + n_chunk])
            y = a + a * jnp.tanh(b)
            g_scr[r0:r0 + sub, c:c + n_chunk] = _dot(hn, win_ref[:, 2 * width + c:2 * width + c + n_chunk])
            for i, ci in enumerate(chunk_ids):
                y_scr[ci, HALO + r0:HALO + r0 + sub, :] = y[:, i * LANES:(i + 1) * LANES]
        for ci in chunk_ids:
            for r in range(0, tm, row_chunk):
                acc = jnp.zeros((row_chunk, LANES), F32)
                for k in range(CV_KERNEL):
                    acc = acc + cw_ref[ci, k:k + 1, :] * y_scr[ci, HALO - (CV_KERNEL - 1) + k + r:
                                                               HALO - (CV_KERNEL - 1) + k + r + row_chunk, :]
                c_scr[ci, r:r + row_chunk, :] = acc + cb_ref[ci]
            y_scr[ci, 0:HALO, :] = y_scr[ci, tm:tm + HALO, :]

    for r0 in starts:
        rows = slice(r0, r0 + sub)
        tot = c_scr[0, rows, :]
        for ci in range(1, n_lane_chunks):
            tot = tot + c_scr[ci, rows, :]
        mu = jnp.sum(tot, axis=-1, keepdims=True) * (1.0 / width)
        d0 = c_scr[0, rows, :] - mu
        sq = d0 * d0
        for ci in range(1, n_lane_chunks):
            dc = c_scr[ci, rows, :] - mu
            sq = sq + dc * dc
        rstd = lax.rsqrt(jnp.sum(sq, axis=-1, keepdims=True) * (1.0 / width) + EPS)

        for ci in range(n_lane_chunks):
            lo = ci * LANES
            yh = (c_scr[ci, rows, :] - mu) * rstd * lng_ref[:, lo:lo + LANES] + lnb_ref[:, lo:lo + LANES]
            og_scr[rows, lo:lo + LANES] = (_silu_of_half(yh)
                                           * _silu_of_half(g_scr[rows, lo:lo + LANES])).astype(BF16)

        out_ref[rows, :] = x_ref[rows, :] + _dot(og_scr[rows, :], wout_ref[...])


def _conv_layer(x3, gain, win, cw, cb, lng, lnb, wout, tm=512, sub=256, n_chunk=256, row_chunk=128):
    batch, seq, d = x3.shape
    width = wout.shape[0]
    n_lane_chunks = width // LANES
    return pl.pallas_call(
        functools.partial(_conv_kernel, sub=sub, n_chunk=n_chunk, row_chunk=row_chunk),
        grid=(batch, seq // tm),
        in_specs=[pl.BlockSpec((None, tm, d), lambda b, t: (b, t, 0)), _resident((1, d)), _resident(win.shape),
                  _resident(cw.shape), _resident(cb.shape), _resident((1, width)), _resident((1, width)),
                  _resident(wout.shape)],
        out_specs=pl.BlockSpec((None, tm, d), lambda b, t: (b, t, 0)),
        out_shape=jax.ShapeDtypeStruct((batch, seq, d), F32),
        scratch_shapes=[pltpu.VMEM((n_lane_chunks, tm + HALO, LANES), F32),
                        pltpu.VMEM((n_lane_chunks, tm, LANES), F32),
                        pltpu.VMEM((tm, width), F32),
                        pltpu.VMEM((tm, width), BF16)],
        compiler_params=pltpu.CompilerParams(dimension_semantics=("arbitrary", "arbitrary"),
                                             vmem_limit_bytes=VMEM_LIMIT),
        name="conv_layer",
    )(x3, gain, win, cw, cb, lng, lnb, wout)


def _scale_q_columns(w_in, width):
    col = lax.broadcasted_iota(jnp.int32, (1, w_in.shape[1]), 1)
    return w_in * jnp.where(col < width, Q_SCALE, 1.0)


def _layer_stick_breaking(x2, batch, seq, gain, w_in, w_out, final_gain, final_norm):
    width = w_out.shape[0]
    proj = _norm_proj(x2, gain[None, :], _scale_q_columns(w_in, width).astype(BF16))
    row = lax.broadcasted_iota(jnp.int32, (TK, TK), 0)
    col = lax.broadcasted_iota(jnp.int32, (TK, TK), 1)
    tri = (row > col).astype(BF16)
    return _attention_layer(_sb_attn_kernel, 2, proj, x2, w_out.astype(BF16), final_gain, final_norm, batch, seq,
                            tri, _resident((TK, TK)), "sb_attention")


def _layer_forgetting(x2, batch, seq, gain, w_in, b_f, w_out, final_gain, final_norm):
    width = w_out.shape[0]
    n_heads = b_f.shape[0]
    d = w_in.shape[0]
    w_main = _scale_q_columns(w_in[:, :4 * width], width).astype(BF16)
    w_f = jnp.zeros((d, LANES), BF16).at[:, :n_heads].set(w_in[:, 4 * width:].astype(BF16))
    b_pad = jnp.zeros((1, LANES), F32).at[0, :n_heads].set(b_f)
    proj, lf = _norm_proj_gate(x2, gain[None, :], w_main, w_f, b_pad)
    cum = _cumsum_seq(lf, seq)
    n_pairs = n_heads // 2
    cum = cum[:, :n_heads].reshape(batch, seq // TK, TK, n_pairs, 2).transpose(0, 3, 1, 4, 2)
    cum_spec = pl.BlockSpec((None, PAIRS, seq // TK, 2, TK), lambda b, i: (b, 0, 0, 0, 0))
    return _attention_layer(_fox_attn_kernel, 3, proj, x2, w_out.astype(BF16), final_gain, final_norm, batch, seq,
                            cum, cum_spec, "fox_attention")


def _layer_gmlp(x2, gain, w_in, ln_g, ln_b, w_s, b_s, w_out):
    width = w_out.shape[0]
    bias_full = jnp.repeat(b_s.T, width // GM_GROUPS, axis=1)
    return _gmlp_layer(x2, gain[None, :], w_in.astype(BF16), ln_g[None, :], ln_b[None, :], w_s, bias_full,
                       w_out.astype(BF16))


def _layer_conv(x2, batch, seq, gain, w_in, conv_w, conv_b, ln_g, ln_b, w_out):
    width = w_out.shape[0]
    n_lane_chunks = width // LANES
    d = x2.shape[1]
    cw = jnp.zeros((HALO, width), F32).at[:CV_KERNEL].set(conv_w)
    cw = cw.reshape(HALO, n_lane_chunks, LANES).transpose(1, 0, 2)
    cb = conv_b.reshape(n_lane_chunks, 1, LANES)
    y = _conv_layer(x2.reshape(batch, seq, d), gain[None, :], (0.5 * w_in).astype(BF16), cw, cb,
                    0.5 * ln_g[None, :], 0.5 * ln_b[None, :], w_out.astype(BF16))
    return y.reshape(batch * seq, d)


def kernel(x, a_norm, a_w_in, a_w_out, b_norm, b_w_in, b_v_ln_g, b_v_ln_b, b_w_s, b_b_s, b_w_out, c_norm, c_w_in, c_conv_w, c_conv_b, c_ln_g, c_ln_b, c_w_out, d_norm, d_w_in, d_b_f, d_w_out, final_norm):
    batch, seq, d = x.shape
    counts = (a_norm.shape[0], b_norm.shape[0], c_norm.shape[0], d_norm.shape[0])
    depth = sum(counts)
    fg = final_norm[None, :]
    x2 = x.reshape(batch * seq, d)
    for i in range(depth):
        kind, j = i % 4, i // 4
        last = i == depth - 1
        if kind == 0:
            x2 = _layer_stick_breaking(x2, batch, seq, a_norm[j], a_w_in[j], a_w_out[j], fg, last)
        elif kind == 1:
            x2 = _layer_gmlp(x2, b_norm[j], b_w_in[j], b_v_ln_g[j], b_v_ln_b[j], b_w_s[j], b_b_s[j], b_w_out[j])
        elif kind == 2:
            x2 = _layer_conv(x2, batch, seq, c_norm[j], c_w_in[j], c_conv_w[j], c_conv_b[j], c_ln_g[j],
                             c_ln_b[j], c_w_out[j])
        else:
            x2 = _layer_forgetting(x2, batch, seq, d_norm[j], d_w_in[j], d_b_f[j], d_w_out[j], fg, last)
    return x2.reshape(batch, seq, d)
```
